```python
import math
import jax, jax.numpy as jnp
from jax import lax
import numpy as np

D_MODEL = 1024
BATCH = 8
SEQ = 4096
DEPTH = 2

MEM_LEN = 256
EPS = 1e-6
ROPE_BASE = 10000.0
RET_HEADS = 4
RET_DK = 128
RET_DV = 256
RET_CHUNK = 128
SWA_Q_HEADS = 16
SWA_KV_HEADS = 2
SWA_HD = 64
SWA_WINDOW = 128
MEM_HEADS = 4
MEM_HD = 256
D_FF = 2816
N_BRANCH = 3

RET_QK_W = RET_HEADS * RET_DK
RET_V_W = RET_HEADS * RET_DV
SWA_Q_W = SWA_Q_HEADS * SWA_HD
SWA_KV_W = SWA_KV_HEADS * SWA_HD
MEM_Q_W = MEM_HEADS * MEM_HD
SPLITS = [RET_QK_W, RET_QK_W, RET_V_W, RET_V_W, SWA_Q_W, SWA_KV_W, SWA_KV_W, MEM_Q_W, N_BRANCH * D_MODEL]
IN_W = sum(SPLITS)
SPLIT_POINTS = [sum(SPLITS[:i + 1]) for i in range(len(SPLITS) - 1)]
NEG_INF = -1e30

kernel_name = "hybrid_retention_swa_sink_memory_macaron"


def rmsnorm(x, g):
    xf = x.astype(jnp.float32)
    y = xf * lax.rsqrt(jnp.mean(xf * xf, axis=-1, keepdims=True) + EPS)
    return (y * g.astype(jnp.float32)).astype(x.dtype)


def swiglu(h, w1, w3, w2):
    return (jax.nn.silu(h @ w1) * (h @ w3)) @ w2


def rotary(x):
    S, d = x.shape[1], x.shape[-1]
    half = d // 2
    inv = ROPE_BASE ** (-jnp.arange(half, dtype=jnp.float32) / half)
    ang = jnp.arange(S, dtype=jnp.float32)[:, None] * inv[None, :]
    cos = jnp.cos(ang)[None, :, None, :]
    sin = jnp.sin(ang)[None, :, None, :]
    x1, x2 = x[..., :half], x[..., half:]
    return jnp.concatenate([x1 * cos - x2 * sin, x1 * sin + x2 * cos], axis=-1)


def retention(q, k, v):
    out_dtype = v.dtype
    B, S, H, dk = q.shape
    dv = v.shape[-1]
    C = RET_CHUNK
    N = S // C
    f32 = jnp.float32
    log_g = jnp.log1p(-jnp.power(2.0, -5.0 - jnp.arange(H, dtype=f32)))
    q = rotary(q.astype(f32))
    k = rotary(k.astype(f32)) * (dk ** -0.5)
    v = v.astype(f32)
    qc = q.reshape(B, N, C, H, dk)
    kc = k.reshape(B, N, C, H, dk)
    vc = v.reshape(B, N, C, H, dv)
    i = jnp.arange(C, dtype=f32)
    diff = i[:, None] - i[None, :]
    intra_decay = jnp.where(diff[None] >= 0,
                            jnp.exp(jnp.maximum(diff, 0.0)[None] * log_g[:, None, None]), 0.0)
    s = jnp.einsum('bnchd,bnshd->bnhcs', qc, kc) * intra_decay
    intra = jnp.einsum('bnhcs,bnshe->bnche', s, vc)
    zeta = jnp.exp((C - 1 - i)[:, None] * log_g[None, :])
    xi = jnp.exp((i + 1)[:, None] * log_g[None, :])
    chunk_state = jnp.einsum('bnchd,bnche->bnhde', kc * zeta[..., None], vc)
    g_chunk = jnp.exp(C * log_g)[:, None, None]

    def step(R, st):
        return R * g_chunk + st, R

    _, prev = lax.scan(step, jnp.zeros((B, H, dk, dv), f32), jnp.moveaxis(chunk_state, 1, 0))
    prev = jnp.moveaxis(prev, 0, 1)
    cross = jnp.einsum('bnchd,bnhde->bnche', qc * xi[..., None], prev)
    return (intra + cross).reshape(B, S, H, dv).astype(out_dtype)


def sliding_window_attention(q, k, v, sinks):
    B, S, Hq, hd = q.shape
    Hkv = k.shape[2]
    G = Hq // Hkv
    W = SWA_WINDOW
    N = S // W
    qb = q.reshape(B, N, W, Hkv, G, hd)
    kb = k.reshape(B, N, W, Hkv, hd)
    vb = v.reshape(B, N, W, Hkv, hd)
    pad = ((0, 0), (1, 0), (0, 0), (0, 0), (0, 0))
    kk = jnp.concatenate([jnp.pad(kb, pad)[:, :-1], kb], axis=2)
    vv = jnp.concatenate([jnp.pad(vb, pad)[:, :-1], vb], axis=2)
    s = jnp.einsum('bnqkgd,bnskd->bnkgqs', qb, kk,
                   preferred_element_type=jnp.float32) * (hd ** -0.5)
    qpos = jnp.arange(W)[:, None] + W
    kpos = jnp.arange(2 * W)[None, :]
    rel = qpos - kpos
    band = (rel >= 0) & (rel < W)
    first = (jnp.arange(N) == 0)[:, None, None] & (kpos < W)[None]
    valid = band[None] & ~first
    s = jnp.where(valid[None, :, None, None], s, NEG_INF)
    sink = sinks.astype(jnp.float32).reshape(Hkv, G)[None, None, :, :, None, None]
    m = jnp.maximum(jnp.max(s, axis=-1, keepdims=True), sink)
    p = jnp.exp(s - m)
    p = p / (jnp.sum(p, axis=-1, keepdims=True) + jnp.exp(sink - m))
    o = jnp.einsum('bnkgqs,bnskd->bnqkgd', p.astype(v.dtype), vv)
    return o.reshape(B, S, Hq * hd)


def memory_attention(q, mk, mv):
    B, S, H, d = q.shape
    s = jnp.einsum('bshd,bmhd->bhsm', q, mk, preferred_element_type=jnp.float32) * (d ** -0.5)
    p = jax.nn.softmax(s, axis=-1)
    o = jnp.einsum('bhsm,bmhd->bshd', p.astype(mv.dtype), mv)
    return o.reshape(B, S, H * d)


def setup_inputs(seed: int = 0) -> dict:
    key = jax.random.key(seed)
    ks = jax.random.split(key, 24)
    L, D, F = DEPTH, D_MODEL, D_FF

    def w(k, shape, fan_in):
        return jax.random.normal(k, shape, jnp.float32) * (fan_in ** -0.5)

    def gain(k, shape):
        return 1.0 + 0.02 * jax.random.normal(k, shape, jnp.float32)

    return {
        "x": jax.random.normal(ks[0], (BATCH, SEQ, D), jnp.float32),
        "mem": jax.random.normal(ks[1], (BATCH, MEM_LEN, D), jnp.float32),
        "norm_ffn1": gain(ks[2], (L, D)),
        "ffn1_w1": w(ks[3], (L, D, F), D),
        "ffn1_w3": w(ks[4], (L, D, F), D),
        "ffn1_w2": w(ks[5], (L, F, D), F),
        "norm_mix": gain(ks[6], (L, D)),
        "w_in": w(ks[7], (L, D, IN_W), D),
        "ret_gn": gain(ks[8], (L, RET_V_W)),
        "swa_sinks": 0.5 * jax.random.normal(ks[9], (L, SWA_Q_HEADS), jnp.float32),
        "norm_mem": gain(ks[10], (L, D)),
        "w_mem_kv": w(ks[11], (L, D, 2 * MEM_Q_W), D),
        "w_up_ret": w(ks[12], (L, RET_V_W, D), RET_V_W),
        "w_up_swa": w(ks[13], (L, SWA_Q_W, D), SWA_Q_W),
        "w_up_mem": w(ks[14], (L, MEM_Q_W, D), MEM_Q_W),
        "w_o": w(ks[15], (L, D, D), D),
        "norm_ffn2": gain(ks[16], (L, D)),
        "ffn2_w1": w(ks[17], (L, D, F), D),
        "ffn2_w3": w(ks[18], (L, D, F), D),
        "ffn2_w2": w(ks[19], (L, F, D), F),
        "norm_final": gain(ks[20], (D,)),
    }


def reference(x, mem, norm_ffn1, ffn1_w1, ffn1_w3, ffn1_w2, norm_mix, w_in, ret_gn, swa_sinks,
              norm_mem, w_mem_kv, w_up_ret, w_up_swa, w_up_mem, w_o, norm_ffn2, ffn2_w1,
              ffn2_w3, ffn2_w2, norm_final):
    B, S, D = x.shape
    for l in range(DEPTH):
        h = rmsnorm(x, norm_ffn1[l])
        x = x + 0.5 * swiglu(h, ffn1_w1[l], ffn1_w3[l], ffn1_w2[l])

        h = rmsnorm(x, norm_mix[l])
        proj = h @ w_in[l]
        rq, rk, rv, rg, sq, sk, sv, mq, gl = jnp.split(proj, SPLIT_POINTS, axis=-1)

        ret = retention(rq.reshape(B, S, RET_HEADS, RET_DK),
                        rk.reshape(B, S, RET_HEADS, RET_DK),
                        rv.reshape(B, S, RET_HEADS, RET_DV))
        ret = rmsnorm(ret, ret_gn[l].reshape(RET_HEADS, RET_DV))
        ret = ret.reshape(B, S, RET_V_W) * jax.nn.silu(rg)

        swa = sliding_window_attention(sq.reshape(B, S, SWA_Q_HEADS, SWA_HD),
                                       sk.reshape(B, S, SWA_KV_HEADS, SWA_HD),
                                       sv.reshape(B, S, SWA_KV_HEADS, SWA_HD),
                                       swa_sinks[l])

        mkv = rmsnorm(mem, norm_mem[l]) @ w_mem_kv[l]
        mk, mv = jnp.split(mkv, 2, axis=-1)
        mo = memory_attention(mq.reshape(B, S, MEM_HEADS, MEM_HD),
                              mk.reshape(B, MEM_LEN, MEM_HEADS, MEM_HD),
                              mv.reshape(B, MEM_LEN, MEM_HEADS, MEM_HD))

        gates = jax.nn.sigmoid(gl.reshape(B, S, N_BRANCH, D))
        merged = (gates[:, :, 0] * (ret @ w_up_ret[l])
                  + gates[:, :, 1] * (swa @ w_up_swa[l])
                  + gates[:, :, 2] * (mo @ w_up_mem[l]))
        x = x + merged @ w_o[l]

        h = rmsnorm(x, norm_ffn2[l])
        x = x + 0.5 * swiglu(h, ffn2_w1[l], ffn2_w3[l], ffn2_w2[l])
    return rmsnorm(x, norm_final)
```

```python
import functools
import math

import jax
import jax.numpy as jnp
from jax import lax
from jax.experimental import pallas as pl
from jax.experimental.pallas import tpu as pltpu

F32 = jnp.float32
BF16 = jnp.bfloat16

D_MODEL = 1024
DEPTH = 2
MEM_LEN = 256
EPS = 1e-6
ROPE_BASE = 10000.0
RET_HEADS = 4
RET_DK = 128
RET_DV = 256
RET_CHUNK = 128
SWA_Q_HEADS = 16
SWA_KV_HEADS = 2
SWA_HD = 64
SWA_WINDOW = 128
MEM_HEADS = 4
MEM_HD = 256
D_FF = 2816
NEG_INF = -1e30

RET_QK_W = RET_HEADS * RET_DK
RET_V_W = RET_HEADS * RET_DV
SWA_Q_W = SWA_Q_HEADS * SWA_HD
SWA_KV_W = SWA_KV_HEADS * SWA_HD
MEM_Q_W = MEM_HEADS * MEM_HD
OFF_RQ = 0
OFF_RK = OFF_RQ + RET_QK_W
OFF_RV = OFF_RK + RET_QK_W
OFF_RG = OFF_RV + RET_V_W
OFF_SQ = OFF_RG + RET_V_W
OFF_SK = OFF_SQ + SWA_Q_W
OFF_SV = OFF_SK + SWA_KV_W
OFF_MQ = OFF_SV + SWA_KV_W
OFF_GL = OFF_MQ + MEM_Q_W

LANES = 128
HEADS_PER_LANE_TILE = LANES // SWA_HD
SWA_GROUP = SWA_Q_HEADS // SWA_KV_HEADS
PAIRS_PER_KV = SWA_GROUP // HEADS_PER_LANE_TILE

FFN_TILE = 512
MIX_TILE = 256
VMEM_LIMIT_BYTES = 56 * 1024 * 1024


def _rmsnorm(x, g):
    return x * lax.rsqrt(jnp.mean(x * x, axis=-1, keepdims=True) + EPS) * g


def _dot(a, b):
    return jnp.dot(a, b, preferred_element_type=F32)


def _dot_nt(a, b):
    return lax.dot_general(a, b, (((1,), (1,)), ((), ())), preferred_element_type=F32)


def _dot_tn(a, b):
    return lax.dot_general(a, b, (((0,), (0,)), ((), ())), preferred_element_type=F32)


def _resident(shape):
    zeros = (0,) * len(shape)
    return pl.BlockSpec(shape, lambda *_: zeros, pipeline_mode=pl.Buffered(1))


def _ffn_kernel(x_ref, g_ref, w1_ref, w3_ref, w2_ref, gf_ref, o_ref, *, final_norm):
    x = x_ref[...]
    h = _rmsnorm(x, g_ref[...]).astype(BF16)
    a = _dot(h, w1_ref[...])
    b = _dot(h, w3_ref[...])
    u = (jax.nn.silu(a) * b).astype(BF16)
    out = x + 0.5 * _dot(u, w2_ref[...])
    if final_norm:
        out = _rmsnorm(out, gf_ref[...])
    o_ref[...] = out


def _ffn(x, g, w1, w3, w2, g_final, final_norm):
    n_tok, d = x.shape
    f = w1.shape[1]
    tile = pl.BlockSpec((FFN_TILE, d), lambda i: (i, 0))
    return pl.pallas_call(
        functools.partial(_ffn_kernel, final_norm=final_norm),
        grid=(n_tok // FFN_TILE,),
        in_specs=[tile, _resident((1, d)), _resident((d, f)), _resident((d, f)), _resident((f, d)),
                  _resident((1, d))],
        out_specs=tile,
        out_shape=jax.ShapeDtypeStruct((n_tok, d), F32),
        compiler_params=pltpu.CompilerParams(dimension_semantics=("arbitrary",),
                                             vmem_limit_bytes=VMEM_LIMIT_BYTES),
        name="ffn_final" if final_norm else "ffn",
    )(x, g.reshape(1, d), w1, w3, w2, g_final.reshape(1, d))


def _memkv_kernel(mem_ref, g_ref, w_ref, k_ref, v_ref):
    h = _rmsnorm(mem_ref[...], g_ref[...]).astype(BF16)
    kv = _dot(h, w_ref[...])
    k_ref[...] = kv[:, :MEM_Q_W].astype(BF16)
    v_ref[...] = kv[:, MEM_Q_W:].astype(BF16)


def _memkv(mem, g, w):
    b, m, d = mem.shape
    out = jax.ShapeDtypeStruct((b, m, MEM_Q_W), BF16)
    return pl.pallas_call(
        _memkv_kernel,
        grid=(b,),
        in_specs=[pl.BlockSpec((None, m, d), lambda i: (i, 0, 0)), _resident((1, d)),
                  _resident((d, 2 * MEM_Q_W))],
        out_specs=[pl.BlockSpec((None, m, MEM_Q_W), lambda i: (i, 0, 0))] * 2,
        out_shape=[out, out],
        compiler_params=pltpu.CompilerParams(dimension_semantics=("arbitrary",),
                                             vmem_limit_bytes=VMEM_LIMIT_BYTES),
        name="mem_kv",
    )(mem, g.reshape(1, d), w)


def _mix_kernel(sinks_ref, x_ref, cos_ref, sin_ref, g_ref, win_ref, decay_ref, xi_ref, zeta_ref, gn_ref,
                mk_ref, mv_ref, wr_ref, ws_ref, wm_ref, wo_ref, o_ref,
                state_scr, kext_scr, vext_scr, ret_scr, swa_scr, mo_scr, *, chunk_decay):
    t = pl.program_id(1)
    tm = x_ref.shape[0]
    n_blk = tm // SWA_WINDOW

    @pl.when(t == 0)
    def _():
        state_scr[...] = jnp.zeros_like(state_scr)
        kext_scr[0:SWA_WINDOW, :] = jnp.zeros((SWA_WINDOW, SWA_KV_W), F32)
        vext_scr[0:SWA_WINDOW, :] = jnp.zeros((SWA_WINDOW, SWA_KV_W), F32)

    x = x_ref[...]
    h = _rmsnorm(x, g_ref[...]).astype(BF16)

    def proj(off, width):
        return _dot(h, win_ref[:, off:off + width])

    cos = cos_ref[...]
    sin = sin_ref[...]

    def rotate(v):
        return v * cos + pltpu.roll(v, RET_DK // 2, 1) * sin

    for hd in range(RET_HEADS):
        qk_cols = slice(hd * RET_DK, (hd + 1) * RET_DK)
        v_cols = slice(hd * RET_DV, (hd + 1) * RET_DV)
        q = rotate(proj(OFF_RQ + hd * RET_DK, RET_DK))
        k = rotate(proj(OFF_RK + hd * RET_DK, RET_DK)) * (RET_DK ** -0.5)
        v = proj(OFF_RV + hd * RET_DV, RET_DV).astype(BF16)
        gate = jax.nn.silu(proj(OFF_RG + hd * RET_DV, RET_DV))
        for c in range(tm // RET_CHUNK):
            rows = slice(c * RET_CHUNK, (c + 1) * RET_CHUNK)
            qc, kc, vc = q[rows], k[rows], v[rows]
            prev = state_scr[hd]
            s = _dot_nt(qc.astype(BF16), kc.astype(BF16)) * decay_ref[hd]
            lhs = jnp.concatenate([s.astype(BF16), (qc * xi_ref[:, qk_cols]).astype(BF16)], axis=1)
            rhs = jnp.concatenate([vc, prev.astype(BF16)], axis=0)
            o = _dot(lhs, rhs)
            state_scr[hd] = prev * chunk_decay[hd] + _dot_tn((kc * zeta_ref[:, qk_cols]).astype(BF16), vc)
            o = _rmsnorm(o, gn_ref[:, v_cols])
            ret_scr[rows, v_cols] = (o * gate[rows]).astype(BF16)

    sq = proj(OFF_SQ, SWA_Q_W).astype(BF16)
    kext_scr[SWA_WINDOW:, :] = proj(OFF_SK, SWA_KV_W)
    vext_scr[SWA_WINDOW:, :] = proj(OFF_SV, SWA_KV_W)
    lane = lax.broadcasted_iota(jnp.int32, (1, LANES), 1)
    low_half = lane < SWA_HD
    qpos = lax.broadcasted_iota(jnp.int32, (SWA_WINDOW, 2 * SWA_WINDOW), 0)
    kpos = lax.broadcasted_iota(jnp.int32, (SWA_WINDOW, 2 * SWA_WINDOW), 1)
    band = (kpos > qpos) & (kpos <= qpos + SWA_WINDOW)
    for j in range(n_blk):
        rows = slice(j * SWA_WINDOW, (j + 1) * SWA_WINDOW)
        ext_rows = slice(j * SWA_WINDOW, (j + 2) * SWA_WINDOW)
        valid = band & jnp.logical_or(t != 0, kpos >= SWA_WINDOW) if j == 0 else band
        kk = kext_scr[ext_rows, :]
        vv = vext_scr[ext_rows, :]
        for kh in range(SWA_KV_HEADS):
            def halves(a):
                own = jnp.where(low_half if kh == 0 else jnp.logical_not(low_half), a, 0.0)
                other = pltpu.roll(own, SWA_HD, 1)
                return (own, other) if kh == 0 else (other, own)

            k_lo, k_hi = halves(kk)
            v_lo, v_hi = halves(vv)
            kcat = jnp.concatenate([k_lo, k_hi], axis=0).astype(BF16)
            vcat = jnp.concatenate([v_lo, v_hi], axis=0).astype(BF16)
            pair0 = kh * PAIRS_PER_KV
            qs = jnp.concatenate(
                [sq[rows, (pair0 + p) * LANES:(pair0 + p + 1) * LANES] for p in range(PAIRS_PER_KV)], axis=0)
            s = _dot_nt(qs, kcat) * (SWA_HD ** -0.5)
            probs = []
            inv = []
            for p in range(PAIRS_PER_KV):
                row_p = []
                inv_p = []
                for e in range(HEADS_PER_LANE_TILE):
                    sb = s[p * SWA_WINDOW:(p + 1) * SWA_WINDOW, e * 2 * SWA_WINDOW:(e + 1) * 2 * SWA_WINDOW]
                    sb = jnp.where(valid, sb, NEG_INF)
                    sink = sinks_ref[(pair0 + p) * HEADS_PER_LANE_TILE + e]
                    m = jnp.maximum(jnp.max(sb, axis=-1, keepdims=True), sink)
                    pe = jnp.exp(sb - m)
                    den = jnp.sum(pe, axis=-1, keepdims=True) + jnp.exp(sink - m)
                    row_p.append(pe.astype(BF16))
                    inv_p.append(1.0 / den)
                probs.append(jnp.concatenate(row_p, axis=1))
                inv.append(inv_p)
            o = _dot(jnp.concatenate(probs, axis=0), vcat)
            for p in range(PAIRS_PER_KV):
                scale = jnp.where(low_half, inv[p][0], inv[p][1])
                op = o[p * SWA_WINDOW:(p + 1) * SWA_WINDOW] * scale
                swa_scr[rows, (pair0 + p) * LANES:(pair0 + p + 1) * LANES] = op.astype(BF16)
    kext_scr[0:SWA_WINDOW, :] = kext_scr[tm:tm + SWA_WINDOW, :]
    vext_scr[0:SWA_WINDOW, :] = vext_scr[tm:tm + SWA_WINDOW, :]

    for hd in range(MEM_HEADS):
        cols = slice(hd * MEM_HD, (hd + 1) * MEM_HD)
        q = proj(OFF_MQ + hd * MEM_HD, MEM_HD).astype(BF16)
        s = _dot_nt(q, mk_ref[:, cols]) * (MEM_HD ** -0.5)
        m = jnp.max(s, axis=-1, keepdims=True)
        pe = jnp.exp(s - m)
        den = jnp.sum(pe, axis=-1, keepdims=True)
        o = _dot(pe.astype(BF16), mv_ref[:, cols]) * (1.0 / den)
        mo_scr[:, cols] = o.astype(BF16)

    merged = jax.nn.sigmoid(proj(OFF_GL, D_MODEL)) * _dot(ret_scr[...], wr_ref[...])
    merged += jax.nn.sigmoid(proj(OFF_GL + D_MODEL, D_MODEL)) * _dot(swa_scr[...], ws_ref[...])
    merged += jax.nn.sigmoid(proj(OFF_GL + 2 * D_MODEL, D_MODEL)) * _dot(mo_scr[...], wm_ref[...])
    o_ref[...] = x + _dot(merged.astype(BF16), wo_ref[...])


def _mix(x, batch, seq, cos, sin, g, w_in, decay, xi, zeta, gn, sinks, mk, mv, wr, ws, wm, wo, chunk_decay):
    n_tok, d = x.shape
    tm = MIX_TILE
    steps = seq // tm
    tok = pl.BlockSpec((tm, d), lambda b, t: (b * steps + t, 0))
    pos = pl.BlockSpec((tm, RET_DK), lambda b, t: (t, 0))
    memb = pl.BlockSpec((None, MEM_LEN, MEM_Q_W), lambda b, t: (b, 0, 0))
    return pl.pallas_call(
        functools.partial(_mix_kernel, chunk_decay=chunk_decay),
        grid=(batch, steps),
        in_specs=[pl.BlockSpec(memory_space=pltpu.SMEM), tok, pos, pos, _resident((1, d)),
                  _resident(w_in.shape), _resident(decay.shape), _resident(xi.shape), _resident(zeta.shape),
                  _resident((1, RET_V_W)), memb, memb,
                  _resident(wr.shape), _resident(ws.shape), _resident(wm.shape), _resident(wo.shape)],
        out_specs=tok,
        out_shape=jax.ShapeDtypeStruct((n_tok, d), F32),
        scratch_shapes=[pltpu.VMEM((RET_HEADS, RET_DK, RET_DV), F32),
                        pltpu.VMEM((tm + SWA_WINDOW, SWA_KV_W), F32),
                        pltpu.VMEM((tm + SWA_WINDOW, SWA_KV_W), F32),
                        pltpu.VMEM((tm, RET_V_W), BF16),
                        pltpu.VMEM((tm, SWA_Q_W), BF16),
                        pltpu.VMEM((tm, MEM_Q_W), BF16)],
        compiler_params=pltpu.CompilerParams(dimension_semantics=("arbitrary", "arbitrary"),
                                             vmem_limit_bytes=VMEM_LIMIT_BYTES),
        name="mix",
    )(sinks, x, cos, sin, g.reshape(1, d), w_in, decay, xi, zeta, gn.reshape(1, RET_V_W), mk, mv, wr, ws, wm, wo)


def _position_tables(seq):
    half = RET_DK // 2
    inv = ROPE_BASE ** (-jnp.arange(half, dtype=F32) / half)
    ang = jnp.arange(seq, dtype=F32)[:, None] * inv[None, :]
    cos, sin = jnp.cos(ang), jnp.sin(ang)
    cos_full = jnp.concatenate([cos, cos], axis=-1)
    sin_full = jnp.concatenate([-sin, sin], axis=-1)
    log_g = jnp.log1p(-jnp.power(2.0, -5.0 - jnp.arange(RET_HEADS, dtype=F32)))
    i = jnp.arange(RET_CHUNK, dtype=F32)
    diff = i[:, None] - i[None, :]
    decay = jnp.where(diff[None] >= 0, jnp.exp(jnp.maximum(diff, 0.0)[None] * log_g[:, None, None]), 0.0)
    zeta = jnp.exp((RET_CHUNK - 1 - i)[:, None] * log_g[None, :])
    xi = jnp.exp((i + 1)[:, None] * log_g[None, :])
    widen = lambda a: jnp.repeat(a, RET_DK, axis=1)
    return cos_full, sin_full, decay, widen(xi), widen(zeta)


def kernel(x, mem, norm_ffn1, ffn1_w1, ffn1_w3, ffn1_w2, norm_mix, w_in, ret_gn, swa_sinks, norm_mem, w_mem_kv,
           w_up_ret, w_up_swa, w_up_mem, w_o, norm_ffn2, ffn2_w1, ffn2_w3, ffn2_w2, norm_final):
    batch, seq, d = x.shape
    assert d == D_MODEL and seq % MIX_TILE == 0 and (batch * seq) % FFN_TILE == 0
    assert MIX_TILE % RET_CHUNK == 0 and MIX_TILE % SWA_WINDOW == 0
    cos, sin, decay, xi, zeta = _position_tables(seq)
    chunk_decay = tuple(math.exp(RET_CHUNK * math.log1p(-(2.0 ** (-5.0 - hd)))) for hd in range(RET_HEADS))
    bf = lambda w: w.astype(BF16)
    xf = x.reshape(batch * seq, d)
    for l in range(DEPTH):
        xf = _ffn(xf, norm_ffn1[l], bf(ffn1_w1[l]), bf(ffn1_w3[l]), bf(ffn1_w2[l]), norm_final, False)
        mk, mv = _memkv(mem, norm_mem[l], bf(w_mem_kv[l]))
        xf = _mix(xf, batch, seq, cos, sin, norm_mix[l], bf(w_in[l]), decay, xi, zeta, ret_gn[l], swa_sinks[l],
                  mk, mv, bf(w_up_ret[l]), bf(w_up_swa[l]), bf(w_up_mem[l]), bf(w_o[l]), chunk_decay)
        xf = _ffn(xf, norm_ffn2[l], bf(ffn2_w1[l]), bf(ffn2_w3[l]), bf(ffn2_w2[l]), norm_final, l == DEPTH - 1)
    return xf.reshape(batch, seq, d)
```

```python
import functools
import math

import jax
import jax.numpy as jnp
from jax import lax
from jax.experimental import pallas as pl
from jax.experimental.pallas import tpu as pltpu

F32 = jnp.float32
BF16 = jnp.bfloat16

D_MODEL = 1024
DEPTH = 2
MEM_LEN = 256
EPS = 1e-6
ROPE_BASE = 10000.0
RET_HEADS = 4
RET_DK = 128
RET_DV = 256
RET_CHUNK = 128
SWA_Q_HEADS = 16
SWA_KV_HEADS = 2
SWA_HD = 64
SWA_WINDOW = 128
MEM_HEADS = 4
MEM_HD = 256
D_FF = 2816
NEG_INF = -1e30

RET_QK_W = RET_HEADS * RET_DK
RET_V_W = RET_HEADS * RET_DV
SWA_Q_W = SWA_Q_HEADS * SWA_HD
SWA_KV_W = SWA_KV_HEADS * SWA_HD
MEM_Q_W = MEM_HEADS * MEM_HD
OFF_RQ = 0
OFF_RK = OFF_RQ + RET_QK_W
OFF_RV = OFF_RK + RET_QK_W
OFF_RG = OFF_RV + RET_V_W
OFF_SQ = OFF_RG + RET_V_W
OFF_SK = OFF_SQ + SWA_Q_W
OFF_SV = OFF_SK + SWA_KV_W
OFF_MQ = OFF_SV + SWA_KV_W
OFF_GL = OFF_MQ + MEM_Q_W

LANES = 128
HEADS_PER_LANE_TILE = LANES // SWA_HD
SWA_GROUP = SWA_Q_HEADS // SWA_KV_HEADS
PAIRS_PER_KV = SWA_GROUP // HEADS_PER_LANE_TILE

FFN_TILE = 512
MIX_TILE = 256
VMEM_LIMIT_BYTES = 56 * 1024 * 1024


def _rmsnorm(x, g):
    return x * lax.rsqrt(jnp.mean(x * x, axis=-1, keepdims=True) + EPS) * g


def _dot(a, b):
    return jnp.dot(a, b, preferred_element_type=F32)


def _dot_nt(a, b):
    return lax.dot_general(a, b, (((1,), (1,)), ((), ())), preferred_element_type=F32)


def _dot_tn(a, b):
    return lax.dot_general(a, b, (((0,), (0,)), ((), ())), preferred_element_type=F32)


def _resident(shape):
    zeros = (0,) * len(shape)
    return pl.BlockSpec(shape, lambda *_: zeros, pipeline_mode=pl.Buffered(1))


def _ffn_kernel(x_ref, g_ref, w1_ref, w3_ref, w2_ref, gf_ref, o_ref, *, final_norm):
    x = x_ref[...]
    h = _rmsnorm(x, g_ref[...]).astype(BF16)
    a = _dot(h, w1_ref[...])
    b = _dot(h, w3_ref[...])
    u = (jax.nn.silu(a) * b).astype(BF16)
    out = x + 0.5 * _dot(u, w2_ref[...])
    if final_norm:
        out = _rmsnorm(out, gf_ref[...])
    o_ref[...] = out


def _ffn(x, g, w1, w3, w2, g_final, final_norm):
    n_tok, d = x.shape
    f = w1.shape[1]
    tile = pl.BlockSpec((FFN_TILE, d), lambda i: (i, 0))
    return pl.pallas_call(
        functools.partial(_ffn_kernel, final_norm=final_norm),
        grid=(n_tok // FFN_TILE,),
        in_specs=[tile, _resident((1, d)), _resident((d, f)), _resident((d, f)), _resident((f, d)),
                  _resident((1, d))],
        out_specs=tile,
        out_shape=jax.ShapeDtypeStruct((n_tok, d), F32),
        compiler_params=pltpu.CompilerParams(dimension_semantics=("arbitrary",),
                                             vmem_limit_bytes=VMEM_LIMIT_BYTES),
        name="ffn_final" if final_norm else "ffn",
    )(x, g.reshape(1, d), w1, w3, w2, g_final.reshape(1, d))


def _memkv_kernel(mem_ref, g_ref, w_ref, k_ref, v_ref):
    h = _rmsnorm(mem_ref[...], g_ref[...]).astype(BF16)
    kv = _dot(h, w_ref[...])
    k_ref[...] = kv[:, :MEM_Q_W].astype(BF16)
    v_ref[...] = kv[:, MEM_Q_W:].astype(BF16)


def _memkv(mem, g, w):
    b, m, d = mem.shape
    out = jax.ShapeDtypeStruct((b, m, MEM_Q_W), BF16)
    return pl.pallas_call(
        _memkv_kernel,
        grid=(b,),
        in_specs=[pl.BlockSpec((None, m, d), lambda i: (i, 0, 0)), _resident((1, d)),
                  _resident((d, 2 * MEM_Q_W))],
        out_specs=[pl.BlockSpec((None, m, MEM_Q_W), lambda i: (i, 0, 0))] * 2,
        out_shape=[out, out],
        compiler_params=pltpu.CompilerParams(dimension_semantics=("arbitrary",),
                                             vmem_limit_bytes=VMEM_LIMIT_BYTES),
        name="mem_kv",
    )(mem, g.reshape(1, d), w)


def _mix_kernel(sinks_ref, x_ref, cos_ref, sin_ref, g_ref, win_ref, decay_ref, xi_ref, zeta_ref, gn_ref,
                mk_ref, mv_ref, wr_ref, ws_ref, wm_ref, wo_ref, o_ref,
                state_scr, kext_scr, vext_scr, ret_scr, swa_scr, mo_scr, *, chunk_decay):
    t = pl.program_id(1)
    tm = x_ref.shape[0]
    n_blk = tm // SWA_WINDOW

    @pl.when(t == 0)
    def _():
        state_scr[...] = jnp.zeros_like(state_scr)
        kext_scr[0:SWA_WINDOW, :] = jnp.zeros((SWA_WINDOW, SWA_KV_W), F32)
        vext_scr[0:SWA_WINDOW, :] = jnp.zeros((SWA_WINDOW, SWA_KV_W), F32)

    x = x_ref[...]
    h = _rmsnorm(x, g_ref[...]).astype(BF16)

    def proj(off, width):
        return _dot(h, win_ref[:, off:off + width])


    cos = cos_ref[...]
    sin = sin_ref[...]
    heads = range(RET_HEADS)
    chunks = [slice(c * RET_CHUNK, (c + 1) * RET_CHUNK) for c in range(tm // RET_CHUNK)]
    qk_cols = [slice(hd * RET_DK, (hd + 1) * RET_DK) for hd in heads]
    v_cols = [slice(hd * RET_DV, (hd + 1) * RET_DV) for hd in heads]

    def rotate(v):
        return v * cos + pltpu.roll(v, RET_DK // 2, 1) * sin

    rq = proj(OFF_RQ, RET_QK_W)
    rk = proj(OFF_RK, RET_QK_W)
    q = [rotate(rq[:, qk_cols[hd]]) for hd in heads]
    k = [rotate(rk[:, qk_cols[hd]]) * (RET_DK ** -0.5) for hd in heads]
    v = proj(OFF_RV, RET_V_W).astype(BF16)
    scores = [[_dot_nt(q[hd][rows].astype(BF16), k[hd][rows].astype(BF16)) for rows in chunks] for hd in heads]
    chunk_state = [[_dot_tn((k[hd][rows] * zeta_ref[:, qk_cols[hd]]).astype(BF16), v[rows, v_cols[hd]])
                    for rows in chunks] for hd in heads]
    gate = jax.nn.silu(proj(OFF_RG, RET_V_W))
    state = [state_scr[hd] for hd in heads]

    def retention_chunk(c):
        rows = chunks[c]
        for hd in heads:
            s = scores[hd][c] * decay_ref[hd]
            lhs = jnp.concatenate([s.astype(BF16), (q[hd][rows] * xi_ref[:, qk_cols[hd]]).astype(BF16)], axis=1)
            rhs = jnp.concatenate([v[rows, v_cols[hd]], state[hd].astype(BF16)], axis=0)
            o = _rmsnorm(_dot(lhs, rhs), gn_ref[:, v_cols[hd]])
            ret_scr[rows, v_cols[hd]] = (o * gate[rows, v_cols[hd]]).astype(BF16)
            state[hd] = state[hd] * chunk_decay[hd] + chunk_state[hd][c]

    retention_chunk(0)
    sq = (proj(OFF_SQ, SWA_Q_W) * (SWA_HD ** -0.5)).astype(BF16)
    skv = proj(OFF_SK, 2 * SWA_KV_W)
    kext_scr[SWA_WINDOW:, :] = skv[:, :SWA_KV_W]
    vext_scr[SWA_WINDOW:, :] = skv[:, SWA_KV_W:]
    for c in range(1, len(chunks)):
        retention_chunk(c)
    for hd in heads:
        state_scr[hd] = state[hd]

    lane = lax.broadcasted_iota(jnp.int32, (1, LANES), 1)
    low_half = lane < SWA_HD
    qpos = lax.broadcasted_iota(jnp.int32, (SWA_WINDOW, 2 * SWA_WINDOW), 0)
    kpos = lax.broadcasted_iota(jnp.int32, (SWA_WINDOW, 2 * SWA_WINDOW), 1)
    band = (kpos > qpos) & (kpos <= qpos + SWA_WINDOW)

    def lane_halves(a, kh):
        own = jnp.where(low_half if kh == 0 else jnp.logical_not(low_half), a, 0.0)
        other = pltpu.roll(own, SWA_HD, 1)
        return (own, other) if kh == 0 else (other, own)

    def swa_scores(j, kh):
        rows = slice(j * SWA_WINDOW, (j + 1) * SWA_WINDOW)
        k_lo, k_hi = lane_halves(kext_scr[j * SWA_WINDOW:(j + 2) * SWA_WINDOW, :], kh)
        kcat = jnp.concatenate([k_lo, k_hi], axis=0).astype(BF16)
        pair0 = kh * PAIRS_PER_KV
        qs = jnp.concatenate(
            [sq[rows, (pair0 + p) * LANES:(pair0 + p + 1) * LANES] for p in range(PAIRS_PER_KV)], axis=0)
        return _dot_nt(qs, kcat)

    def swa_output(j, kh, s):
        rows = slice(j * SWA_WINDOW, (j + 1) * SWA_WINDOW)
        valid = band & jnp.logical_or(t != 0, kpos >= SWA_WINDOW) if j == 0 else band
        v_lo, v_hi = lane_halves(vext_scr[j * SWA_WINDOW:(j + 2) * SWA_WINDOW, :], kh)
        vcat = jnp.concatenate([v_lo, v_hi], axis=0).astype(BF16)
        pair0 = kh * PAIRS_PER_KV
        probs = []
        inv = []
        for p in range(PAIRS_PER_KV):
            row_p = []
            inv_p = []
            for e in range(HEADS_PER_LANE_TILE):
                sb = s[p * SWA_WINDOW:(p + 1) * SWA_WINDOW, e * 2 * SWA_WINDOW:(e + 1) * 2 * SWA_WINDOW]
                sb = jnp.where(valid, sb, NEG_INF)
                sink = sinks_ref[(pair0 + p) * HEADS_PER_LANE_TILE + e]
                m = jnp.maximum(jnp.max(sb, axis=-1, keepdims=True), sink)
                pe = jnp.exp(sb - m)
                den = jnp.sum(pe, axis=-1, keepdims=True) + jnp.exp(sink - m)
                row_p.append(pe.astype(BF16))
                inv_p.append(1.0 / den)
            probs.append(jnp.concatenate(row_p, axis=1))
            inv.append(inv_p)
        o = _dot(jnp.concatenate(probs, axis=0), vcat)
        for p in range(PAIRS_PER_KV):
            scale = jnp.where(low_half, inv[p][0], inv[p][1])
            op = o[p * SWA_WINDOW:(p + 1) * SWA_WINDOW] * scale
            swa_scr[rows, (pair0 + p) * LANES:(pair0 + p + 1) * LANES] = op.astype(BF16)

    m_cols = [slice(hd * MEM_HD, (hd + 1) * MEM_HD) for hd in range(MEM_HEADS)]

    def mem_output(cols, s):
        m = jnp.max(s, axis=-1, keepdims=True)
        pe = jnp.exp(s - m)
        den = jnp.sum(pe, axis=-1, keepdims=True)
        o = _dot(pe.astype(BF16), mv_ref[:, cols]) * (1.0 / den)
        mo_scr[:, cols] = o.astype(BF16)

    items = [(j, kh) for j in range(n_blk) for kh in range(SWA_KV_HEADS)]
    swa_s = [swa_scores(j, kh) for j, kh in items]
    mq = (proj(OFF_MQ, MEM_Q_W) * (MEM_HD ** -0.5)).astype(BF16)
    merged = jax.nn.sigmoid(proj(OFF_GL, D_MODEL)) * _dot(ret_scr[...], wr_ref[...])
    late = {}
    fillers = [
        lambda: late.update(mem_s=[_dot_nt(mq[:, cols], mk_ref[:, cols]) for cols in m_cols]),
        lambda: late.update(gate_swa=jax.nn.sigmoid(proj(OFF_GL + D_MODEL, D_MODEL))),
        lambda: late.update(gate_mem=jax.nn.sigmoid(proj(OFF_GL + 2 * D_MODEL, D_MODEL))),
    ]
    for (j, kh), s in zip(items, swa_s):
        swa_output(j, kh, s)
        if fillers:
            fillers.pop(0)()
    while fillers:
        fillers.pop(0)()
    kext_scr[0:SWA_WINDOW, :] = kext_scr[tm:tm + SWA_WINDOW, :]
    vext_scr[0:SWA_WINDOW, :] = vext_scr[tm:tm + SWA_WINDOW, :]
    for cols, s in zip(m_cols, late["mem_s"]):
        mem_output(cols, s)

    merged += late["gate_swa"] * _dot(swa_scr[...], ws_ref[...])
    merged += late["gate_mem"] * _dot(mo_scr[...], wm_ref[...])
    o_ref[...] = x + _dot(merged.astype(BF16), wo_ref[...])


def _mix(x, batch, seq, cos, sin, g, w_in, decay, xi, zeta, gn, sinks, mk, mv, wr, ws, wm, wo, chunk_decay):
    n_tok, d = x.shape
    tm = MIX_TILE
    steps = seq // tm
    tok = pl.BlockSpec((tm, d), lambda b, t: (b * steps + t, 0))
    pos = pl.BlockSpec((tm, RET_DK), lambda b, t: (t, 0))
    memb = pl.BlockSpec((None, MEM_LEN, MEM_Q_W), lambda b, t: (b, 0, 0))
    return pl.pallas_call(
        functools.partial(_mix_kernel, chunk_decay=chunk_decay),
        grid=(batch, steps),
        in_specs=[pl.BlockSpec(memory_space=pltpu.SMEM), tok, pos, pos, _resident((1, d)),
                  _resident(w_in.shape), _resident(decay.shape), _resident(xi.shape), _resident(zeta.shape),
                  _resident((1, RET_V_W)), memb, memb,
                  _resident(wr.shape), _resident(ws.shape), _resident(wm.shape), _resident(wo.shape)],
        out_specs=tok,
        out_shape=jax.ShapeDtypeStruct((n_tok, d), F32),
        scratch_shapes=[pltpu.VMEM((RET_HEADS, RET_DK, RET_DV), F32),
                        pltpu.VMEM((tm + SWA_WINDOW, SWA_KV_W), F32),
                        pltpu.VMEM((tm + SWA_WINDOW, SWA_KV_W), F32),
                        pltpu.VMEM((tm, RET_V_W), BF16),
                        pltpu.VMEM((tm, SWA_Q_W), BF16),
                        pltpu.VMEM((tm, MEM_Q_W), BF16)],
        compiler_params=pltpu.CompilerParams(dimension_semantics=("arbitrary", "arbitrary"),
                                             vmem_limit_bytes=VMEM_LIMIT_BYTES),
        name="mix",
    )(sinks, x, cos, sin, g.reshape(1, d), w_in, decay, xi, zeta, gn.reshape(1, RET_V_W), mk, mv, wr, ws, wm, wo)


def _position_tables(seq):
    half = RET_DK // 2
    inv = ROPE_BASE ** (-jnp.arange(half, dtype=F32) / half)
    ang = jnp.arange(seq, dtype=F32)[:, None] * inv[None, :]
    cos, sin = jnp.cos(ang), jnp.sin(ang)
    cos_full = jnp.concatenate([cos, cos], axis=-1)
    sin_full = jnp.concatenate([-sin, sin], axis=-1)
    log_g = jnp.log1p(-jnp.power(2.0, -5.0 - jnp.arange(RET_HEADS, dtype=F32)))
    i = jnp.arange(RET_CHUNK, dtype=F32)
    diff = i[:, None] - i[None, :]
    decay = jnp.where(diff[None] >= 0, jnp.exp(jnp.maximum(diff, 0.0)[None] * log_g[:, None, None]), 0.0)
    zeta = jnp.exp((RET_CHUNK - 1 - i)[:, None] * log_g[None, :])
    xi = jnp.exp((i + 1)[:, None] * log_g[None, :])
    widen = lambda a: jnp.repeat(a, RET_DK, axis=1)
    return cos_full, sin_full, decay, widen(xi), widen(zeta)


def kernel(x, mem, norm_ffn1, ffn1_w1, ffn1_w3, ffn1_w2, norm_mix, w_in, ret_gn, swa_sinks, norm_mem, w_mem_kv,
           w_up_ret, w_up_swa, w_up_mem, w_o, norm_ffn2, ffn2_w1, ffn2_w3, ffn2_w2, norm_final):
    batch, seq, d = x.shape
    assert d == D_MODEL and seq % MIX_TILE == 0 and (batch * seq) % FFN_TILE == 0
    assert MIX_TILE % RET_CHUNK == 0 and MIX_TILE % SWA_WINDOW == 0
    cos, sin, decay, xi, zeta = _position_tables(seq)
    chunk_decay = tuple(math.exp(RET_CHUNK * math.log1p(-(2.0 ** (-5.0 - hd)))) for hd in range(RET_HEADS))
    bf = lambda w: w.astype(BF16)
    xf = x.reshape(batch * seq, d)
    for l in range(DEPTH):
        xf = _ffn(xf, norm_ffn1[l], bf(ffn1_w1[l]), bf(ffn1_w3[l]), bf(ffn1_w2[l]), norm_final, False)
        mk, mv = _memkv(mem, norm_mem[l], bf(w_mem_kv[l]))
        xf = _mix(xf, batch, seq, cos, sin, norm_mix[l], bf(w_in[l]), decay, xi, zeta, ret_gn[l], swa_sinks[l],
                  mk, mv, bf(w_up_ret[l]), bf(w_up_swa[l]), bf(w_up_mem[l]), bf(w_o[l]), chunk_decay)
        xf = _ffn(xf, norm_ffn2[l], bf(ffn2_w1[l]), bf(ffn2_w3[l]), bf(ffn2_w2[l]), norm_final, l == DEPTH - 1)
    return xf.reshape(batch, seq, d)
```

```python
import functools
import math

import jax
import jax.numpy as jnp
from jax import lax
from jax.experimental import pallas as pl
from jax.experimental.pallas import tpu as pltpu

F32 = jnp.float32
BF16 = jnp.bfloat16

D_MODEL = 1024
DEPTH = 2
MEM_LEN = 256
EPS = 1e-6
ROPE_BASE = 10000.0
RET_HEADS = 4
RET_DK = 128
RET_DV = 256
RET_CHUNK = 128
SWA_Q_HEADS = 16
SWA_KV_HEADS = 2
SWA_HD = 64
SWA_WINDOW = 128
MEM_HEADS = 4
MEM_HD = 256
D_FF = 2816
NEG_INF = -1e30

RET_QK_W = RET_HEADS * RET_DK
RET_V_W = RET_HEADS * RET_DV
SWA_Q_W = SWA_Q_HEADS * SWA_HD
SWA_KV_W = SWA_KV_HEADS * SWA_HD
MEM_Q_W = MEM_HEADS * MEM_HD
OFF_RQ = 0
OFF_RK = OFF_RQ + RET_QK_W
OFF_RV = OFF_RK + RET_QK_W
OFF_RG = OFF_RV + RET_V_W
OFF_SQ = OFF_RG + RET_V_W
OFF_SK = OFF_SQ + SWA_Q_W
OFF_SV = OFF_SK + SWA_KV_W
OFF_MQ = OFF_SV + SWA_KV_W
OFF_GL = OFF_MQ + MEM_Q_W

LANES = 128
MXU_WIDTH = 256
HEADS_PER_LANE_TILE = LANES // SWA_HD
SWA_GROUP = SWA_Q_HEADS // SWA_KV_HEADS
PAIRS_PER_KV = SWA_GROUP // HEADS_PER_LANE_TILE

FFN_TILE = 512
FFN_TILES_PER_STEP = 2
MIX_TILE = 256
MIX_TILES_PER_STEP = 2
VMEM_LIMIT_BYTES = 56 * 1024 * 1024


def _rmsnorm(x, g):
    return x * lax.rsqrt(jnp.mean(x * x, axis=-1, keepdims=True) + EPS) * g


def _dot(a, b):
    return jnp.dot(a, b, preferred_element_type=F32)


def _dot_nt(a, b):
    return lax.dot_general(a, b, (((1,), (1,)), ((), ())), preferred_element_type=F32)


def _dot_tn(a, b):
    return lax.dot_general(a, b, (((0,), (0,)), ((), ())), preferred_element_type=F32)


def _resident(shape):
    zeros = (0,) * len(shape)
    return pl.BlockSpec(shape, lambda *_: zeros, pipeline_mode=pl.Buffered(1))


def _layer(stacked, l):
    zeros = (0,) * (stacked.ndim - 1)
    return pl.BlockSpec((None,) + stacked.shape[1:], lambda *_: (l,) + zeros, pipeline_mode=pl.Buffered(1))


def _ffn_kernel(x_ref, g_ref, w1_ref, w3_ref, w2_ref, gf_ref, o_ref, *, final_norm):
    for s in range(x_ref.shape[0] // FFN_TILE):
        rows = slice(s * FFN_TILE, (s + 1) * FFN_TILE)
        x = x_ref[rows, :]
        h = _rmsnorm(x, g_ref[...]).astype(BF16)
        a = _dot(h, w1_ref[...])
        b = _dot(h, w3_ref[...])
        u = (jax.nn.silu(a) * b).astype(BF16)
        out = x + 0.5 * _dot(u, w2_ref[...])
        if final_norm:
            out = _rmsnorm(out, gf_ref[...])
        o_ref[rows, :] = out


def _ffn(x, g, w1, w3, w2, l, g_final, final_norm):
    n_tok, d = x.shape
    block_rows = FFN_TILE * FFN_TILES_PER_STEP
    block = pl.BlockSpec((block_rows, d), lambda i: (i, 0))
    return pl.pallas_call(
        functools.partial(_ffn_kernel, final_norm=final_norm),
        grid=(n_tok // block_rows,),
        in_specs=[block, _layer(g, l), _layer(w1, l), _layer(w3, l), _layer(w2, l), _resident((1, d))],
        out_specs=block,
        out_shape=jax.ShapeDtypeStruct((n_tok, d), F32),
        compiler_params=pltpu.CompilerParams(dimension_semantics=("arbitrary",),
                                             vmem_limit_bytes=VMEM_LIMIT_BYTES),
        name="ffn_final" if final_norm else "ffn",
    )(x, g, w1, w3, w2, g_final.reshape(1, d))


def _memkv_kernel(mem_ref, g_ref, w_ref, k_ref, v_ref):
    h = _rmsnorm(mem_ref[...], g_ref[...]).astype(BF16)
    kv = _dot(h, w_ref[...])
    k_ref[...] = kv[:, :MEM_Q_W].astype(BF16)
    v_ref[...] = kv[:, MEM_Q_W:].astype(BF16)


def _memkv(mem, g, w, l):
    b, m, d = mem.shape
    out = jax.ShapeDtypeStruct((b, m, MEM_Q_W), BF16)
    return pl.pallas_call(
        _memkv_kernel,
        grid=(b,),
        in_specs=[pl.BlockSpec((None, m, d), lambda i: (i, 0, 0)), _layer(g, l), _layer(w, l)],
        out_specs=[pl.BlockSpec((None, m, MEM_Q_W), lambda i: (i, 0, 0))] * 2,
        out_shape=[out, out],
        compiler_params=pltpu.CompilerParams(dimension_semantics=("arbitrary",),
                                             vmem_limit_bytes=VMEM_LIMIT_BYTES),
        name="mem_kv",
    )(mem, g, w)


def _mix_kernel(sinks_ref, x_ref, cos_ref, sin_ref, g_ref, win_ref, decay_ref, xi_ref, zeta_ref, gn_ref,
                mk_ref, mv_ref, wr_ref, ws_ref, wm_ref, wo_ref, o_ref,
                state_scr, kext_scr, vext_scr, ret_scr, swa_scr, mo_scr, *, layer, chunk_decay):
    block_rows = x_ref.shape[0]

    @pl.when(pl.program_id(1) == 0)
    def _():
        state_scr[...] = jnp.zeros_like(state_scr)
        kext_scr[0:SWA_WINDOW, :] = jnp.zeros((SWA_WINDOW, SWA_KV_W), F32)
        vext_scr[0:SWA_WINDOW, :] = jnp.zeros((SWA_WINDOW, SWA_KV_W), F32)

    for s in range(block_rows // MIX_TILE):
        _mix_tile(s, sinks_ref, x_ref, cos_ref, sin_ref, g_ref, win_ref, decay_ref, xi_ref, zeta_ref, gn_ref,
                  mk_ref, mv_ref, wr_ref, ws_ref, wm_ref, wo_ref, o_ref,
                  state_scr, kext_scr, vext_scr, ret_scr, swa_scr, mo_scr, layer=layer, chunk_decay=chunk_decay)
    kext_scr[0:SWA_WINDOW, :] = kext_scr[block_rows:block_rows + SWA_WINDOW, :]
    vext_scr[0:SWA_WINDOW, :] = vext_scr[block_rows:block_rows + SWA_WINDOW, :]


def _mix_tile(s, sinks_ref, x_ref, cos_ref, sin_ref, g_ref, win_ref, decay_ref, xi_ref, zeta_ref, gn_ref,
              mk_ref, mv_ref, wr_ref, ws_ref, wm_ref, wo_ref, o_ref,
              state_scr, kext_scr, vext_scr, ret_scr, swa_scr, mo_scr, *, layer, chunk_decay):
    t = pl.program_id(1)
    tm = MIX_TILE
    r0 = s * tm
    tile = slice(r0, r0 + tm)
    in_block = lambda rows: slice(r0 + rows.start, r0 + rows.stop)
    n_blk = tm // SWA_WINDOW

    x = x_ref[tile, :]
    h = _rmsnorm(x, g_ref[...]).astype(BF16)

    def proj(off, width):
        return _dot(h, win_ref[:, off:off + width])


    cos = cos_ref[tile, :]
    sin = sin_ref[tile, :]
    heads = range(RET_HEADS)
    chunks = [slice(c * RET_CHUNK, (c + 1) * RET_CHUNK) for c in range(tm // RET_CHUNK)]
    qk_cols = [slice(hd * RET_DK, (hd + 1) * RET_DK) for hd in heads]
    v_cols = [slice(hd * RET_DV, (hd + 1) * RET_DV) for hd in heads]

    def rotate(v):
        return v * cos + pltpu.roll(v, RET_DK // 2, 1) * sin

    rq = proj(OFF_RQ, RET_QK_W)
    rk = proj(OFF_RK, RET_QK_W)
    q = [rotate(rq[:, qk_cols[hd]]) for hd in heads]
    k = [rotate(rk[:, qk_cols[hd]]) * (RET_DK ** -0.5) for hd in heads]
    v = proj(OFF_RV, RET_V_W).astype(BF16)
    scores = [[_dot_nt(q[hd][rows].astype(BF16), k[hd][rows].astype(BF16)) for rows in chunks] for hd in heads]
    chunk_state = [[_dot_tn((k[hd][rows] * zeta_ref[:, qk_cols[hd]]).astype(BF16), v[rows, v_cols[hd]])
                    for rows in chunks] for hd in heads]
    gate = jax.nn.silu(proj(OFF_RG, RET_V_W))
    state = [state_scr[hd] for hd in heads]

    def retention_chunk(c):
        rows = chunks[c]
        for hd in heads:
            s = scores[hd][c] * decay_ref[hd]
            lhs = jnp.concatenate([s.astype(BF16), (q[hd][rows] * xi_ref[:, qk_cols[hd]]).astype(BF16)], axis=1)
            rhs = jnp.concatenate([v[rows, v_cols[hd]], state[hd].astype(BF16)], axis=0)
            o = _rmsnorm(_dot(lhs, rhs), gn_ref[:, v_cols[hd]])
            ret_scr[in_block(rows), v_cols[hd]] = (o * gate[rows, v_cols[hd]]).astype(BF16)
            state[hd] = state[hd] * chunk_decay[hd] + chunk_state[hd][c]

    retention_chunk(0)
    sq = (proj(OFF_SQ, SWA_Q_W) * (SWA_HD ** -0.5)).astype(BF16)
    skv = proj(OFF_SK, 2 * SWA_KV_W)
    kext_scr[SWA_WINDOW + r0:SWA_WINDOW + r0 + tm, :] = skv[:, :SWA_KV_W]
    vext_scr[SWA_WINDOW + r0:SWA_WINDOW + r0 + tm, :] = skv[:, SWA_KV_W:]
    for c in range(1, len(chunks)):
        retention_chunk(c)
    for hd in heads:
        state_scr[hd] = state[hd]

    lane = lax.broadcasted_iota(jnp.int32, (1, LANES), 1)
    low_half = lane < SWA_HD
    qpos = lax.broadcasted_iota(jnp.int32, (SWA_WINDOW, 2 * SWA_WINDOW), 0)
    kpos = lax.broadcasted_iota(jnp.int32, (SWA_WINDOW, 2 * SWA_WINDOW), 1)
    band = (kpos > qpos) & (kpos <= qpos + SWA_WINDOW)

    def lane_halves(a, kh):
        own = jnp.where(low_half if kh == 0 else jnp.logical_not(low_half), a, 0.0)
        other = pltpu.roll(own, SWA_HD, 1)
        return (own, other) if kh == 0 else (other, own)

    def swa_scores(j, kh):
        rows = slice(j * SWA_WINDOW, (j + 1) * SWA_WINDOW)
        k_lo, k_hi = lane_halves(kext_scr[r0 + j * SWA_WINDOW:r0 + (j + 2) * SWA_WINDOW, :], kh)
        kcat = jnp.concatenate([k_lo, k_hi], axis=0).astype(BF16)
        pair0 = kh * PAIRS_PER_KV
        qs = jnp.concatenate(
            [sq[rows, (pair0 + p) * LANES:(pair0 + p + 1) * LANES] for p in range(PAIRS_PER_KV)], axis=0)
        return _dot_nt(qs, kcat)

    def swa_output(j, kh, scores_jk):
        rows = slice(j * SWA_WINDOW, (j + 1) * SWA_WINDOW)
        valid = band & jnp.logical_or(t != 0, kpos >= SWA_WINDOW) if (s == 0 and j == 0) else band
        v_lo, v_hi = lane_halves(vext_scr[r0 + j * SWA_WINDOW:r0 + (j + 2) * SWA_WINDOW, :], kh)
        vcat = jnp.concatenate([v_lo, v_hi], axis=0).astype(BF16)
        pair0 = kh * PAIRS_PER_KV
        probs = []
        inv = []
        for p in range(PAIRS_PER_KV):
            row_p = []
            inv_p = []
            for e in range(HEADS_PER_LANE_TILE):
                sb = scores_jk[p * SWA_WINDOW:(p + 1) * SWA_WINDOW, e * 2 * SWA_WINDOW:(e + 1) * 2 * SWA_WINDOW]
                sb = jnp.where(valid, sb, NEG_INF)
                sink = sinks_ref[layer, (pair0 + p) * HEADS_PER_LANE_TILE + e]
                m = jnp.maximum(jnp.max(sb, axis=-1, keepdims=True), sink)
                pe = jnp.exp(sb - m)
                den = jnp.sum(pe, axis=-1, keepdims=True) + jnp.exp(sink - m)
                row_p.append(pe.astype(BF16))
                inv_p.append(1.0 / den)
            probs.append(jnp.concatenate(row_p, axis=1))
            inv.append(inv_p)
        o = _dot(jnp.concatenate(probs, axis=0), vcat)
        for p in range(PAIRS_PER_KV):
            scale = jnp.where(low_half, inv[p][0], inv[p][1])
            op = o[p * SWA_WINDOW:(p + 1) * SWA_WINDOW] * scale
            swa_scr[in_block(rows), (pair0 + p) * LANES:(pair0 + p + 1) * LANES] = op.astype(BF16)

    m_cols = [slice(hd * MEM_HD, (hd + 1) * MEM_HD) for hd in range(MEM_HEADS)]

    def mem_output(cols, s):
        m = jnp.max(s, axis=-1, keepdims=True)
        pe = jnp.exp(s - m)
        den = jnp.sum(pe, axis=-1, keepdims=True)
        o = _dot(pe.astype(BF16), mv_ref[:, cols]) * (1.0 / den)
        mo_scr[tile, cols] = o.astype(BF16)

    items = [(j, kh) for j in range(n_blk) for kh in range(SWA_KV_HEADS)]
    late = {}

    def mem_queries():
        mq = (proj(OFF_MQ, MEM_Q_W) * (MEM_HD ** -0.5)).astype(BF16)
        late["mem_s"] = [_dot_nt(mq[:, cols], mk_ref[:, cols]) for cols in m_cols]

    fillers = [
        mem_queries,
        lambda: late.update(up_ret=_dot(ret_scr[tile, :], wr_ref[...])),
        lambda: late.update(gate_ret=jax.nn.sigmoid(proj(OFF_GL, D_MODEL))),
        lambda: late.update(gate_swa=jax.nn.sigmoid(proj(OFF_GL + D_MODEL, D_MODEL))),
        lambda: late.update(gate_mem=jax.nn.sigmoid(proj(OFF_GL + 2 * D_MODEL, D_MODEL))),
    ]

    def fill():
        if fillers:
            fillers.pop(0)()

    swa_s = {0: swa_scores(*items[0])}
    fill()
    for i, (j, kh) in enumerate(items):
        if i + 1 < len(items):
            swa_s[i + 1] = swa_scores(*items[i + 1])
        fill()
        swa_output(j, kh, swa_s.pop(i))
    while fillers:
        fill()
    for cols, sc in zip(m_cols, late["mem_s"]):
        mem_output(cols, sc)

    merged = late["gate_ret"] * late["up_ret"]
    merged += late["gate_swa"] * _dot(swa_scr[tile, :], ws_ref[...])
    merged += late["gate_mem"] * _dot(mo_scr[tile, :], wm_ref[...])
    o_ref[tile, :] = x + _dot(merged.astype(BF16), wo_ref[...])


def _mix(x, batch, seq, l, cos, sin, g, w_in, decay, xi, zeta, gn, sinks, mk, mv, wr, ws, wm, wo, chunk_decay):
    n_tok, d = x.shape
    tm = MIX_TILE * MIX_TILES_PER_STEP
    steps = seq // tm
    tok = pl.BlockSpec((tm, d), lambda b, t: (b * steps + t, 0))
    pos = pl.BlockSpec((tm, RET_DK), lambda b, t: (t, 0))
    memb = pl.BlockSpec((None, MEM_LEN, MEM_Q_W), lambda b, t: (b, 0, 0))
    return pl.pallas_call(
        functools.partial(_mix_kernel, layer=l, chunk_decay=chunk_decay),
        grid=(batch, steps),
        in_specs=[pl.BlockSpec(memory_space=pltpu.SMEM), tok, pos, pos, _layer(g, l), _layer(w_in, l),
                  _resident(decay.shape), _resident(xi.shape), _resident(zeta.shape), _layer(gn, l), memb, memb,
                  _layer(wr, l), _layer(ws, l), _layer(wm, l), _layer(wo, l)],
        out_specs=tok,
        out_shape=jax.ShapeDtypeStruct((n_tok, d), F32),
        scratch_shapes=[pltpu.VMEM((RET_HEADS, RET_DK, RET_DV), F32),
                        pltpu.VMEM((tm + SWA_WINDOW, SWA_KV_W), F32),
                        pltpu.VMEM((tm + SWA_WINDOW, SWA_KV_W), F32),
                        pltpu.VMEM((tm, RET_V_W), BF16),
                        pltpu.VMEM((tm, SWA_Q_W), BF16),
                        pltpu.VMEM((tm, MEM_Q_W), BF16)],
        compiler_params=pltpu.CompilerParams(dimension_semantics=("arbitrary", "arbitrary"),
                                             vmem_limit_bytes=VMEM_LIMIT_BYTES),
        name="mix",
    )(sinks, x, cos, sin, g, w_in, decay, xi, zeta, gn, mk, mv, wr, ws, wm, wo)


def _position_tables(seq):
    half = RET_DK // 2
    inv = ROPE_BASE ** (-jnp.arange(half, dtype=F32) / half)
    ang = jnp.arange(seq, dtype=F32)[:, None] * inv[None, :]
    cos, sin = jnp.cos(ang), jnp.sin(ang)
    cos_full = jnp.concatenate([cos, cos], axis=-1)
    sin_full = jnp.concatenate([-sin, sin], axis=-1)
    log_g = jnp.log1p(-jnp.power(2.0, -5.0 - jnp.arange(RET_HEADS, dtype=F32)))
    i = jnp.arange(RET_CHUNK, dtype=F32)
    diff = i[:, None] - i[None, :]
    decay = jnp.where(diff[None] >= 0, jnp.exp(jnp.maximum(diff, 0.0)[None] * log_g[:, None, None]), 0.0)
    zeta = jnp.exp((RET_CHUNK - 1 - i)[:, None] * log_g[None, :])
    xi = jnp.exp((i + 1)[:, None] * log_g[None, :])
    widen = lambda a: jnp.repeat(a, RET_DK, axis=1)
    return cos_full, sin_full, decay, widen(xi), widen(zeta)


def kernel(x, mem, norm_ffn1, ffn1_w1, ffn1_w3, ffn1_w2, norm_mix, w_in, ret_gn, swa_sinks, norm_mem, w_mem_kv,
           w_up_ret, w_up_swa, w_up_mem, w_o, norm_ffn2, ffn2_w1, ffn2_w3, ffn2_w2, norm_final):
    batch, seq, d = x.shape
    assert d == D_MODEL and seq % (MIX_TILE * MIX_TILES_PER_STEP) == 0
    assert (batch * seq) % (FFN_TILE * FFN_TILES_PER_STEP) == 0
    assert MIX_TILE % RET_CHUNK == 0 and MIX_TILE % SWA_WINDOW == 0
    cos, sin, decay, xi, zeta = _position_tables(seq)
    chunk_decay = tuple(math.exp(RET_CHUNK * math.log1p(-(2.0 ** (-5.0 - hd)))) for hd in range(RET_HEADS))
    bf = lambda w: w.astype(BF16)
    row = lambda g: g.reshape(DEPTH, 1, g.shape[-1])
    ffn1 = (row(norm_ffn1), bf(ffn1_w1), bf(ffn1_w3), bf(ffn1_w2))
    ffn2 = (row(norm_ffn2), bf(ffn2_w1), bf(ffn2_w3), bf(ffn2_w2))
    mix_w = (bf(w_up_ret), bf(w_up_swa), bf(w_up_mem), bf(w_o))
    w_in_b, w_kv_b = bf(w_in), bf(w_mem_kv)
    xf = x.reshape(batch * seq, d)
    for l in range(DEPTH):
        xf = _ffn(xf, *ffn1, l, norm_final, False)
        mk, mv = _memkv(mem, row(norm_mem), w_kv_b, l)
        xf = _mix(xf, batch, seq, l, cos, sin, row(norm_mix), w_in_b, decay, xi, zeta, row(ret_gn), swa_sinks,
                  mk, mv, *mix_w, chunk_decay)
        xf = _ffn(xf, *ffn2, l, norm_final, l == DEPTH - 1)
    return xf.reshape(batch, seq, d)
```

```python
import functools
import math

import jax
import jax.numpy as jnp
from jax import lax
from jax.experimental import pallas as pl
from jax.experimental.pallas import tpu as pltpu

F32 = jnp.float32
BF16 = jnp.bfloat16

D_MODEL = 1024
DEPTH = 2
MEM_LEN = 256
EPS = 1e-6
ROPE_BASE = 10000.0
RET_HEADS = 4
RET_DK = 128
RET_DV = 256
RET_CHUNK = 128
SWA_Q_HEADS = 16
SWA_KV_HEADS = 2
SWA_HD = 64
SWA_WINDOW = 128
MEM_HEADS = 4
MEM_HD = 256
D_FF = 2816
NEG_INF = -1e30

RET_QK_W = RET_HEADS * RET_DK
RET_V_W = RET_HEADS * RET_DV
SWA_Q_W = SWA_Q_HEADS * SWA_HD
SWA_KV_W = SWA_KV_HEADS * SWA_HD
MEM_Q_W = MEM_HEADS * MEM_HD
OFF_RQ = 0
OFF_RK = OFF_RQ + RET_QK_W
OFF_RV = OFF_RK + RET_QK_W
OFF_RG = OFF_RV + RET_V_W
OFF_SQ = OFF_RG + RET_V_W
OFF_SK = OFF_SQ + SWA_Q_W
OFF_SV = OFF_SK + SWA_KV_W
OFF_MQ = OFF_SV + SWA_KV_W
OFF_GL = OFF_MQ + MEM_Q_W

LANES = 128
MXU_WIDTH = 256
HEADS_PER_LANE_TILE = LANES // SWA_HD
SWA_GROUP = SWA_Q_HEADS // SWA_KV_HEADS
PAIRS_PER_KV = SWA_GROUP // HEADS_PER_LANE_TILE

FFN_TILE = 512
FFN_TILES_PER_STEP = 2
MIX_TILE = 256
MIX_TILES_PER_STEP = 2
VMEM_LIMIT_BYTES = 56 * 1024 * 1024


def _rmsnorm(x, g):
    return x * lax.rsqrt(jnp.mean(x * x, axis=-1, keepdims=True) + EPS) * g


def _dot(a, b):
    return jnp.dot(a, b, preferred_element_type=F32)


def _dot_nt(a, b):
    return lax.dot_general(a, b, (((1,), (1,)), ((), ())), preferred_element_type=F32)


def _dot_tn(a, b):
    return lax.dot_general(a, b, (((0,), (0,)), ((), ())), preferred_element_type=F32)


def _resident(shape):
    zeros = (0,) * len(shape)
    return pl.BlockSpec(shape, lambda *_: zeros, pipeline_mode=pl.Buffered(1))


def _layer(stacked, l):
    zeros = (0,) * (stacked.ndim - 1)
    return pl.BlockSpec((None,) + stacked.shape[1:], lambda *_: (l,) + zeros, pipeline_mode=pl.Buffered(1))


def _ffn_kernel(x_ref, g_ref, w1_ref, w3_ref, w2_ref, gf_ref, o_ref, *, final_norm):
    for s in range(x_ref.shape[0] // FFN_TILE):
        rows = slice(s * FFN_TILE, (s + 1) * FFN_TILE)
        x = x_ref[rows, :]
        h = _rmsnorm(x, g_ref[...]).astype(BF16)
        a = _dot(h, w1_ref[...])
        b = _dot(h, w3_ref[...])
        u = (jax.nn.silu(a) * b).astype(BF16)
        out = x + 0.5 * _dot(u, w2_ref[...])
        if final_norm:
            out = _rmsnorm(out, gf_ref[...])
        o_ref[rows, :] = out


def _ffn(x, g, w1, w3, w2, l, g_final, final_norm):
    n_tok, d = x.shape
    block_rows = FFN_TILE * FFN_TILES_PER_STEP
    block = pl.BlockSpec((block_rows, d), lambda i: (i, 0))
    return pl.pallas_call(
        functools.partial(_ffn_kernel, final_norm=final_norm),
        grid=(n_tok // block_rows,),
        in_specs=[block, _layer(g, l), _layer(w1, l), _layer(w3, l), _layer(w2, l), _resident((1, d))],
        out_specs=block,
        out_shape=jax.ShapeDtypeStruct((n_tok, d), F32),
        compiler_params=pltpu.CompilerParams(dimension_semantics=("arbitrary",),
                                             vmem_limit_bytes=VMEM_LIMIT_BYTES),
        name="ffn_final" if final_norm else "ffn",
    )(x, g, w1, w3, w2, g_final.reshape(1, d))


def _memkv_kernel(mem_ref, g_ref, w_ref, k_ref, v_ref):
    h = _rmsnorm(mem_ref[...], g_ref[...]).astype(BF16)
    kv = _dot(h, w_ref[...])
    k_ref[...] = kv[:, :MEM_Q_W].astype(BF16)
    v_ref[...] = kv[:, MEM_Q_W:].astype(BF16)


def _memkv(mem, g, w, l):
    b, m, d = mem.shape
    out = jax.ShapeDtypeStruct((b, m, MEM_Q_W), BF16)
    return pl.pallas_call(
        _memkv_kernel,
        grid=(b,),
        in_specs=[pl.BlockSpec((None, m, d), lambda i: (i, 0, 0)), _layer(g, l), _layer(w, l)],
        out_specs=[pl.BlockSpec((None, m, MEM_Q_W), lambda i: (i, 0, 0))] * 2,
        out_shape=[out, out],
        compiler_params=pltpu.CompilerParams(dimension_semantics=("arbitrary",),
                                             vmem_limit_bytes=VMEM_LIMIT_BYTES),
        name="mem_kv",
    )(mem, g, w)


def _mix_kernel(sinks_ref, x_ref, cos_ref, sin_ref, g_ref, win_ref, decay_ref, xi_ref, zeta_ref, gn_ref,
                mk_ref, mv_ref, wr_ref, ws_ref, wm_ref, wo_ref, o_ref,
                state_scr, kext_scr, vext_scr, ret_scr, swa_scr, mo_scr, *, layer, chunk_decay):
    block_rows = x_ref.shape[0]

    @pl.when(pl.program_id(1) == 0)
    def _():
        state_scr[...] = jnp.zeros_like(state_scr)
        kext_scr[0:SWA_WINDOW, :] = jnp.zeros((SWA_WINDOW, SWA_KV_W), F32)
        vext_scr[0:SWA_WINDOW, :] = jnp.zeros((SWA_WINDOW, SWA_KV_W), F32)

    for s in range(block_rows // MIX_TILE):
        _mix_tile(s, sinks_ref, x_ref, cos_ref, sin_ref, g_ref, win_ref, decay_ref, xi_ref, zeta_ref, gn_ref,
                  mk_ref, mv_ref, wr_ref, ws_ref, wm_ref, wo_ref, o_ref,
                  state_scr, kext_scr, vext_scr, ret_scr, swa_scr, mo_scr, layer=layer, chunk_decay=chunk_decay)
    kext_scr[0:SWA_WINDOW, :] = kext_scr[block_rows:block_rows + SWA_WINDOW, :]
    vext_scr[0:SWA_WINDOW, :] = vext_scr[block_rows:block_rows + SWA_WINDOW, :]


def _mix_tile(s, sinks_ref, x_ref, cos_ref, sin_ref, g_ref, win_ref, decay_ref, xi_ref, zeta_ref, gn_ref,
              mk_ref, mv_ref, wr_ref, ws_ref, wm_ref, wo_ref, o_ref,
              state_scr, kext_scr, vext_scr, ret_scr, swa_scr, mo_scr, *, layer, chunk_decay):
    t = pl.program_id(1)
    tm = MIX_TILE
    r0 = s * tm
    tile = slice(r0, r0 + tm)
    in_block = lambda rows: slice(r0 + rows.start, r0 + rows.stop)
    n_blk = tm // SWA_WINDOW

    x = x_ref[tile, :]
    h = _rmsnorm(x, g_ref[...]).astype(BF16)

    def proj(off, width):
        return _dot(h, win_ref[:, off:off + width])


    cos = cos_ref[tile, :]
    sin = sin_ref[tile, :]
    heads = range(RET_HEADS)
    chunks = [slice(c * RET_CHUNK, (c + 1) * RET_CHUNK) for c in range(tm // RET_CHUNK)]
    qk_cols = [slice(hd * RET_DK, (hd + 1) * RET_DK) for hd in heads]
    v_cols = [slice(hd * RET_DV, (hd + 1) * RET_DV) for hd in heads]

    def rotate(v):
        return v * cos + pltpu.roll(v, RET_DK // 2, 1) * sin

    rq = proj(OFF_RQ, RET_QK_W)
    rk = proj(OFF_RK, RET_QK_W)
    q = [rotate(rq[:, qk_cols[hd]]) for hd in heads]
    k = [rotate(rk[:, qk_cols[hd]]) * (RET_DK ** -0.5) for hd in heads]
    v = proj(OFF_RV, RET_V_W).astype(BF16)
    scores = [[_dot_nt(q[hd][rows].astype(BF16), k[hd][rows].astype(BF16)) for rows in chunks] for hd in heads]
    chunk_state = [[_dot_tn((k[hd][rows] * zeta_ref[:, qk_cols[hd]]).astype(BF16), v[rows, v_cols[hd]])
                    for rows in chunks] for hd in heads]
    gate = jax.nn.silu(proj(OFF_RG, RET_V_W))
    state = [state_scr[hd] for hd in heads]

    def retention_chunk(c):
        rows = chunks[c]
        for hd in heads:
            s = scores[hd][c] * decay_ref[hd]
            lhs = jnp.concatenate([s.astype(BF16), (q[hd][rows] * xi_ref[:, qk_cols[hd]]).astype(BF16)], axis=1)
            rhs = jnp.concatenate([v[rows, v_cols[hd]], state[hd].astype(BF16)], axis=0)
            o = _rmsnorm(_dot(lhs, rhs), gn_ref[:, v_cols[hd]])
            ret_scr[in_block(rows), v_cols[hd]] = (o * gate[rows, v_cols[hd]]).astype(BF16)
            state[hd] = state[hd] * chunk_decay[hd] + chunk_state[hd][c]

    retention_chunk(0)
    sq = (proj(OFF_SQ, SWA_Q_W) * (SWA_HD ** -0.5)).astype(BF16)
    skv = proj(OFF_SK, 2 * SWA_KV_W)
    kext_scr[SWA_WINDOW + r0:SWA_WINDOW + r0 + tm, :] = skv[:, :SWA_KV_W]
    vext_scr[SWA_WINDOW + r0:SWA_WINDOW + r0 + tm, :] = skv[:, SWA_KV_W:]
    for c in range(1, len(chunks)):
        retention_chunk(c)
    for hd in heads:
        state_scr[hd] = state[hd]

    lane = lax.broadcasted_iota(jnp.int32, (1, LANES), 1)
    low_half = lane < SWA_HD
    from_prev = (lax.broadcasted_iota(jnp.int32, (SWA_WINDOW, SWA_WINDOW), 1)
                 > lax.broadcasted_iota(jnp.int32, (SWA_WINDOW, SWA_WINDOW), 0))

    def lane_halves(a, kh):
        own = jnp.where(low_half if kh == 0 else jnp.logical_not(low_half), a, 0.0)
        other = pltpu.roll(own, SWA_HD, 1)
        return (own, other) if kh == 0 else (other, own)

    def swa_scores(j, kh):
        rows = slice(j * SWA_WINDOW, (j + 1) * SWA_WINDOW)
        k_lo, k_hi = lane_halves(kext_scr[r0 + j * SWA_WINDOW:r0 + (j + 2) * SWA_WINDOW, :], kh)
        kcat = jnp.concatenate([k_lo, k_hi], axis=0).astype(BF16)
        pair0 = kh * PAIRS_PER_KV
        qs = jnp.concatenate(
            [sq[rows, (pair0 + p) * LANES:(pair0 + p + 1) * LANES] for p in range(PAIRS_PER_KV)], axis=0)
        return _dot_nt(qs, kcat)

    def swa_output(j, kh, scores_jk):
        rows = slice(j * SWA_WINDOW, (j + 1) * SWA_WINDOW)
        v_lo, v_hi = lane_halves(vext_scr[r0 + j * SWA_WINDOW:r0 + (j + 2) * SWA_WINDOW, :], kh)
        vcat = jnp.concatenate([v_lo, v_hi], axis=0).astype(BF16)
        pair0 = kh * PAIRS_PER_KV
        probs = []
        inv = []
        for p in range(PAIRS_PER_KV):
            row_p = []
            inv_p = []
            for e in range(HEADS_PER_LANE_TILE):
                sb = scores_jk[p * SWA_WINDOW:(p + 1) * SWA_WINDOW, e * 2 * SWA_WINDOW:(e + 1) * 2 * SWA_WINDOW]
                s_prev, s_own = sb[:, :SWA_WINDOW], sb[:, SWA_WINDOW:]
                if s == 0 and j == 0:
                    s_prev = jnp.where(t != 0, s_prev, NEG_INF)
                dense = jnp.where(from_prev, s_prev, s_own)
                sink = sinks_ref[layer, (pair0 + p) * HEADS_PER_LANE_TILE + e]
                m = jnp.maximum(jnp.max(dense, axis=-1, keepdims=True), sink)
                pe = jnp.exp(dense - m)
                den = jnp.sum(pe, axis=-1, keepdims=True) + jnp.exp(sink - m)
                row_p.append(jnp.where(from_prev, pe, 0.0).astype(BF16))
                row_p.append(jnp.where(from_prev, 0.0, pe).astype(BF16))
                inv_p.append(1.0 / den)
            probs.append(jnp.concatenate(row_p, axis=1))
            inv.append(inv_p)
        o = _dot(jnp.concatenate(probs, axis=0), vcat)
        for p in range(PAIRS_PER_KV):
            scale = jnp.where(low_half, inv[p][0], inv[p][1])
            op = o[p * SWA_WINDOW:(p + 1) * SWA_WINDOW] * scale
            swa_scr[in_block(rows), (pair0 + p) * LANES:(pair0 + p + 1) * LANES] = op.astype(BF16)

    m_cols = [slice(hd * MEM_HD, (hd + 1) * MEM_HD) for hd in range(MEM_HEADS)]

    def mem_output(cols, s):
        m = jnp.max(s, axis=-1, keepdims=True)
        pe = jnp.exp(s - m)
        den = jnp.sum(pe, axis=-1, keepdims=True)
        o = _dot(pe.astype(BF16), mv_ref[:, cols]) * (1.0 / den)
        mo_scr[tile, cols] = o.astype(BF16)

    items = [(j, kh) for j in range(n_blk) for kh in range(SWA_KV_HEADS)]
    late = {}

    def mem_queries():
        mq = (proj(OFF_MQ, MEM_Q_W) * (MEM_HD ** -0.5)).astype(BF16)
        late["mem_s"] = [_dot_nt(mq[:, cols], mk_ref[:, cols]) for cols in m_cols]

    fillers = [
        mem_queries,
        lambda: late.update(up_ret=_dot(ret_scr[tile, :], wr_ref[...])),
        lambda: late.update(gate_ret=jax.nn.sigmoid(proj(OFF_GL, D_MODEL))),
        lambda: late.update(gate_swa=jax.nn.sigmoid(proj(OFF_GL + D_MODEL, D_MODEL))),
        lambda: late.update(gate_mem=jax.nn.sigmoid(proj(OFF_GL + 2 * D_MODEL, D_MODEL))),
    ]

    def fill():
        if fillers:
            fillers.pop(0)()

    swa_s = {0: swa_scores(*items[0])}
    fill()
    for i, (j, kh) in enumerate(items):
        if i + 1 < len(items):
            swa_s[i + 1] = swa_scores(*items[i + 1])
        fill()
        swa_output(j, kh, swa_s.pop(i))
    while fillers:
        fill()
    for cols, sc in zip(m_cols, late["mem_s"]):
        mem_output(cols, sc)

    merged = late["gate_ret"] * late["up_ret"]
    merged += late["gate_swa"] * _dot(swa_scr[tile, :], ws_ref[...])
    merged += late["gate_mem"] * _dot(mo_scr[tile, :], wm_ref[...])
    o_ref[tile, :] = x + _dot(merged.astype(BF16), wo_ref[...])


def _mix(x, batch, seq, l, cos, sin, g, w_in, decay, xi, zeta, gn, sinks, mk, mv, wr, ws, wm, wo, chunk_decay):
    n_tok, d = x.shape
    tm = MIX_TILE * MIX_TILES_PER_STEP
    steps = seq // tm
    tok = pl.BlockSpec((tm, d), lambda b, t: (b * steps + t, 0))
    pos = pl.BlockSpec((tm, RET_DK), lambda b, t: (t, 0))
    memb = pl.BlockSpec((None, MEM_LEN, MEM_Q_W), lambda b, t: (b, 0, 0))
    return pl.pallas_call(
        functools.partial(_mix_kernel, layer=l, chunk_decay=chunk_decay),
        grid=(batch, steps),
        in_specs=[pl.BlockSpec(memory_space=pltpu.SMEM), tok, pos, pos, _layer(g, l), _layer(w_in, l),
                  _resident(decay.shape), _resident(xi.shape), _resident(zeta.shape), _layer(gn, l), memb, memb,
                  _layer(wr, l), _layer(ws, l), _layer(wm, l), _layer(wo, l)],
        out_specs=tok,
        out_shape=jax.ShapeDtypeStruct((n_tok, d), F32),
        scratch_shapes=[pltpu.VMEM((RET_HEADS, RET_DK, RET_DV), F32),
                        pltpu.VMEM((tm + SWA_WINDOW, SWA_KV_W), F32),
                        pltpu.VMEM((tm + SWA_WINDOW, SWA_KV_W), F32),
                        pltpu.VMEM((tm, RET_V_W), BF16),
                        pltpu.VMEM((tm, SWA_Q_W), BF16),
                        pltpu.VMEM((tm, MEM_Q_W), BF16)],
        compiler_params=pltpu.CompilerParams(dimension_semantics=("arbitrary", "arbitrary"),
                                             vmem_limit_bytes=VMEM_LIMIT_BYTES),
        name="mix",
    )(sinks, x, cos, sin, g, w_in, decay, xi, zeta, gn, mk, mv, wr, ws, wm, wo)


def _position_tables(seq):
    half = RET_DK // 2
    inv = ROPE_BASE ** (-jnp.arange(half, dtype=F32) / half)
    ang = jnp.arange(seq, dtype=F32)[:, None] * inv[None, :]
    cos, sin = jnp.cos(ang), jnp.sin(ang)
    cos_full = jnp.concatenate([cos, cos], axis=-1)
    sin_full = jnp.concatenate([-sin, sin], axis=-1)
    log_g = jnp.log1p(-jnp.power(2.0, -5.0 - jnp.arange(RET_HEADS, dtype=F32)))
    i = jnp.arange(RET_CHUNK, dtype=F32)
    diff = i[:, None] - i[None, :]
    decay = jnp.where(diff[None] >= 0, jnp.exp(jnp.maximum(diff, 0.0)[None] * log_g[:, None, None]), 0.0)
    zeta = jnp.exp((RET_CHUNK - 1 - i)[:, None] * log_g[None, :])
    xi = jnp.exp((i + 1)[:, None] * log_g[None, :])
    widen = lambda a: jnp.repeat(a, RET_DK, axis=1)
    return cos_full, sin_full, decay, widen(xi), widen(zeta)


def kernel(x, mem, norm_ffn1, ffn1_w1, ffn1_w3, ffn1_w2, norm_mix, w_in, ret_gn, swa_sinks, norm_mem, w_mem_kv,
           w_up_ret, w_up_swa, w_up_mem, w_o, norm_ffn2, ffn2_w1, ffn2_w3, ffn2_w2, norm_final):
    batch, seq, d = x.shape
    assert d == D_MODEL and seq % (MIX_TILE * MIX_TILES_PER_STEP) == 0
    assert (batch * seq) % (FFN_TILE * FFN_TILES_PER_STEP) == 0
    assert MIX_TILE % RET_CHUNK == 0 and MIX_TILE % SWA_WINDOW == 0
    cos, sin, decay, xi, zeta = _position_tables(seq)
    chunk_decay = tuple(math.exp(RET_CHUNK * math.log1p(-(2.0 ** (-5.0 - hd)))) for hd in range(RET_HEADS))
    bf = lambda w: w.astype(BF16)
    row = lambda g: g.reshape(DEPTH, 1, g.shape[-1])
    ffn1 = (row(norm_ffn1), bf(ffn1_w1), bf(ffn1_w3), bf(ffn1_w2))
    ffn2 = (row(norm_ffn2), bf(ffn2_w1), bf(ffn2_w3), bf(ffn2_w2))
    mix_w = (bf(w_up_ret), bf(w_up_swa), bf(w_up_mem), bf(w_o))
    w_in_b, w_kv_b = bf(w_in), bf(w_mem_kv)
    xf = x.reshape(batch * seq, d)
    for l in range(DEPTH):
        xf = _ffn(xf, *ffn1, l, norm_final, False)
        mk, mv = _memkv(mem, row(norm_mem), w_kv_b, l)
        xf = _mix(xf, batch, seq, l, cos, sin, row(norm_mix), w_in_b, decay, xi, zeta, row(ret_gn), swa_sinks,
                  mk, mv, *mix_w, chunk_decay)
        xf = _ffn(xf, *ffn2, l, norm_final, l == DEPTH - 1)
    return xf.reshape(batch, seq, d)
```

```python
import functools
import math

import jax
import jax.numpy as jnp
from jax import lax
from jax.experimental import pallas as pl
from jax.experimental.pallas import tpu as pltpu

F32 = jnp.float32
BF16 = jnp.bfloat16

D_MODEL = 1024
DEPTH = 2
MEM_LEN = 256
EPS = 1e-6
ROPE_BASE = 10000.0
RET_HEADS = 4
RET_DK = 128
RET_DV = 256
RET_CHUNK = 128
SWA_Q_HEADS = 16
SWA_KV_HEADS = 2
SWA_HD = 64
SWA_WINDOW = 128
MEM_HEADS = 4
MEM_HD = 256
D_FF = 2816
NEG_INF = -1e30

RET_QK_W = RET_HEADS * RET_DK
RET_V_W = RET_HEADS * RET_DV
SWA_Q_W = SWA_Q_HEADS * SWA_HD
SWA_KV_W = SWA_KV_HEADS * SWA_HD
MEM_Q_W = MEM_HEADS * MEM_HD
OFF_RQ = 0
OFF_RK = OFF_RQ + RET_QK_W
OFF_RV = OFF_RK + RET_QK_W
OFF_RG = OFF_RV + RET_V_W
OFF_SQ = OFF_RG + RET_V_W
OFF_SK = OFF_SQ + SWA_Q_W
OFF_SV = OFF_SK + SWA_KV_W
OFF_MQ = OFF_SV + SWA_KV_W
OFF_GL = OFF_MQ + MEM_Q_W

LANES = 128
MXU_WIDTH = 256
HEADS_PER_LANE_TILE = LANES // SWA_HD
SWA_GROUP = SWA_Q_HEADS // SWA_KV_HEADS
PAIRS_PER_KV = SWA_GROUP // HEADS_PER_LANE_TILE

FFN_TILE = 512
FFN_TILES_PER_STEP = 2
MIX_TILE = 256
MIX_TILES_PER_STEP = 2
VMEM_LIMIT_BYTES = 56 * 1024 * 1024
WEIGHT_CHUNKS = 8


def _inv_rms(x):
    return lax.rsqrt(jnp.mean(x * x, axis=-1, keepdims=True) + EPS)


def _rmsnorm(x, g):
    return x * _inv_rms(x) * g


def _dot(a, b):
    return jnp.dot(a, b, preferred_element_type=F32)


def _dot_nt(a, b):
    return lax.dot_general(a, b, (((1,), (1,)), ((), ())), preferred_element_type=F32)


def _dot_tn(a, b):
    return lax.dot_general(a, b, (((0,), (0,)), ((), ())), preferred_element_type=F32)


def _resident(shape):
    zeros = (0,) * len(shape)
    return pl.BlockSpec(shape, lambda *_: zeros, pipeline_mode=pl.Buffered(1))


def _layer(stacked, l):
    zeros = (0,) * (stacked.ndim - 1)
    return pl.BlockSpec((None,) + stacked.shape[1:], lambda *_: (l,) + zeros, pipeline_mode=pl.Buffered(1))


_HBM = pl.BlockSpec(memory_space=pl.ANY)


def _stage(rows, cols):
    return pltpu.VMEM((2, rows, cols), F32)


def _load_as_bf16(src_hbm, dst, stage, sem):
    chunk = stage.shape[1]
    n_chunks = src_hbm.shape[0] // chunk
    assert n_chunks * chunk == src_hbm.shape[0] and stage.shape[2] == src_hbm.shape[1]

    def copy(c):
        return pltpu.make_async_copy(src_hbm.at[pl.ds(c * chunk, chunk), :], stage.at[c % 2], sem.at[c % 2])

    copy(0).start()
    for c in range(n_chunks):
        if c + 1 < n_chunks:
            copy(c + 1).start()
        copy(c).wait()
        dst[c * chunk:(c + 1) * chunk, :] = stage[c % 2].astype(BF16)


def _ffn_kernel(x_ref, g_ref, w1_hbm, w3_hbm, w2_hbm, gf_ref, o_ref,
                w1_ref, w3_ref, w2_ref, stage_in, stage_out, sem, *, layer, final_norm):
    @pl.when(pl.program_id(0) == 0)
    def _():
        _load_as_bf16(w1_hbm.at[layer], w1_ref, stage_in, sem)
        _load_as_bf16(w3_hbm.at[layer], w3_ref, stage_in, sem)
        _load_as_bf16(w2_hbm.at[layer], w2_ref, stage_out, sem)

    for s in range(x_ref.shape[0] // FFN_TILE):
        rows = slice(s * FFN_TILE, (s + 1) * FFN_TILE)
        x = x_ref[rows, :]
        h = (x * g_ref[...]).astype(BF16)
        r = _inv_rms(x)
        a = _dot(h, w1_ref[...]) * r
        b = _dot(h, w3_ref[...]) * r
        u = (jax.nn.silu(a) * b).astype(BF16)
        out = x + 0.5 * _dot(u, w2_ref[...])
        if final_norm:
            out = _rmsnorm(out, gf_ref[...])
        o_ref[rows, :] = out


def _ffn(x, g, w1, w3, w2, l, g_final, final_norm):
    n_tok, d = x.shape
    block_rows = FFN_TILE * FFN_TILES_PER_STEP
    block = pl.BlockSpec((block_rows, d), lambda i: (i, 0))
    f = w1.shape[2]
    return pl.pallas_call(
        functools.partial(_ffn_kernel, layer=l, final_norm=final_norm),
        grid=(n_tok // block_rows,),
        in_specs=[block, _layer(g, l), _HBM, _HBM, _HBM, _resident((1, d))],
        out_specs=block,
        out_shape=jax.ShapeDtypeStruct((n_tok, d), F32),
        scratch_shapes=[pltpu.VMEM((d, f), BF16), pltpu.VMEM((d, f), BF16), pltpu.VMEM((f, d), BF16),
                        _stage(d // WEIGHT_CHUNKS, f), _stage(f // WEIGHT_CHUNKS, d),
                        pltpu.SemaphoreType.DMA((2,))],
        compiler_params=pltpu.CompilerParams(dimension_semantics=("arbitrary",),
                                             vmem_limit_bytes=VMEM_LIMIT_BYTES),
        name="ffn_final" if final_norm else "ffn",
    )(x, g, w1, w3, w2, g_final.reshape(1, d))


def _memkv_kernel(mem_ref, g_ref, w_ref, k_ref, v_ref):
    h = _rmsnorm(mem_ref[...], g_ref[...]).astype(BF16)
    kv = _dot(h, w_ref[...])
    k_ref[...] = kv[:, :MEM_Q_W].astype(BF16)
    v_ref[...] = kv[:, MEM_Q_W:].astype(BF16)


def _memkv(mem, g, w, l):
    b, m, d = mem.shape
    out = jax.ShapeDtypeStruct((b, m, MEM_Q_W), BF16)
    return pl.pallas_call(
        _memkv_kernel,
        grid=(b,),
        in_specs=[pl.BlockSpec((None, m, d), lambda i: (i, 0, 0)), _layer(g, l), _layer(w, l)],
        out_specs=[pl.BlockSpec((None, m, MEM_Q_W), lambda i: (i, 0, 0))] * 2,
        out_shape=[out, out],
        compiler_params=pltpu.CompilerParams(dimension_semantics=("arbitrary",),
                                             vmem_limit_bytes=VMEM_LIMIT_BYTES),
        name="mem_kv",
    )(mem, g, w)


def _mix_kernel(sinks_ref, x_ref, cos_ref, sin_ref, g_ref, win_hbm, decay_ref, xi_ref, zeta_ref, gn_ref,
                mk_ref, mv_ref, wr_hbm, ws_hbm, wm_hbm, wo_hbm, o_ref,
                win_ref, wr_ref, ws_ref, wm_ref, wo_ref, stage_in, stage_up, sem,
                state_scr, kext_scr, vext_scr, ret_scr, swa_scr, mo_scr, *, layer, chunk_decay):
    block_rows = x_ref.shape[0]

    @pl.when(jnp.logical_and(pl.program_id(0) == 0, pl.program_id(1) == 0))
    def _():
        _load_as_bf16(win_hbm.at[layer], win_ref, stage_in, sem)
        for src, dst in ((wr_hbm, wr_ref), (ws_hbm, ws_ref), (wm_hbm, wm_ref), (wo_hbm, wo_ref)):
            _load_as_bf16(src.at[layer], dst, stage_up, sem)

    @pl.when(pl.program_id(1) == 0)
    def _():
        state_scr[...] = jnp.zeros_like(state_scr)
        kext_scr[0:SWA_WINDOW, :] = jnp.zeros((SWA_WINDOW, SWA_KV_W), F32)
        vext_scr[0:SWA_WINDOW, :] = jnp.zeros((SWA_WINDOW, SWA_KV_W), F32)

    for s in range(block_rows // MIX_TILE):
        _mix_tile(s, sinks_ref, x_ref, cos_ref, sin_ref, g_ref, win_ref, decay_ref, xi_ref, zeta_ref, gn_ref,
                  mk_ref, mv_ref, wr_ref, ws_ref, wm_ref, wo_ref, o_ref,
                  state_scr, kext_scr, vext_scr, ret_scr, swa_scr, mo_scr, layer=layer, chunk_decay=chunk_decay)
    kext_scr[0:SWA_WINDOW, :] = kext_scr[block_rows:block_rows + SWA_WINDOW, :]
    vext_scr[0:SWA_WINDOW, :] = vext_scr[block_rows:block_rows + SWA_WINDOW, :]


def _mix_tile(s, sinks_ref, x_ref, cos_ref, sin_ref, g_ref, win_ref, decay_ref, xi_ref, zeta_ref, gn_ref,
              mk_ref, mv_ref, wr_ref, ws_ref, wm_ref, wo_ref, o_ref,
              state_scr, kext_scr, vext_scr, ret_scr, swa_scr, mo_scr, *, layer, chunk_decay):
    t = pl.program_id(1)
    tm = MIX_TILE
    r0 = s * tm
    tile = slice(r0, r0 + tm)
    in_block = lambda rows: slice(r0 + rows.start, r0 + rows.stop)
    n_blk = tm // SWA_WINDOW

    x = x_ref[tile, :]
    h = _rmsnorm(x, g_ref[...]).astype(BF16)

    def proj(off, width):
        return _dot(h, win_ref[:, off:off + width])


    cos = cos_ref[tile, :]
    sin = sin_ref[tile, :]
    heads = range(RET_HEADS)
    chunks = [slice(c * RET_CHUNK, (c + 1) * RET_CHUNK) for c in range(tm // RET_CHUNK)]
    qk_cols = [slice(hd * RET_DK, (hd + 1) * RET_DK) for hd in heads]
    v_cols = [slice(hd * RET_DV, (hd + 1) * RET_DV) for hd in heads]

    def rotate(v):
        return v * cos + pltpu.roll(v, RET_DK // 2, 1) * sin

    rq = proj(OFF_RQ, RET_QK_W)
    rk = proj(OFF_RK, RET_QK_W)
    q = [rotate(rq[:, qk_cols[hd]]) for hd in heads]
    k = [rotate(rk[:, qk_cols[hd]]) * (RET_DK ** -0.5) for hd in heads]
    v = proj(OFF_RV, RET_V_W).astype(BF16)
    scores = [[_dot_nt(q[hd][rows].astype(BF16), k[hd][rows].astype(BF16)) for rows in chunks] for hd in heads]
    chunk_state = [[_dot_tn((k[hd][rows] * zeta_ref[:, qk_cols[hd]]).astype(BF16), v[rows, v_cols[hd]])
                    for rows in chunks] for hd in heads]
    gate = jax.nn.silu(proj(OFF_RG, RET_V_W))
    state = [state_scr[hd] for hd in heads]

    def retention_chunk(c):
        rows = chunks[c]
        for hd in heads:
            s = scores[hd][c] * decay_ref[hd]
            lhs = jnp.concatenate([s.astype(BF16), (q[hd][rows] * xi_ref[:, qk_cols[hd]]).astype(BF16)], axis=1)
            rhs = jnp.concatenate([v[rows, v_cols[hd]], state[hd].astype(BF16)], axis=0)
            o = _rmsnorm(_dot(lhs, rhs), gn_ref[:, v_cols[hd]])
            ret_scr[in_block(rows), v_cols[hd]] = (o * gate[rows, v_cols[hd]]).astype(BF16)
            state[hd] = state[hd] * chunk_decay[hd] + chunk_state[hd][c]

    retention_chunk(0)
    sq = (proj(OFF_SQ, SWA_Q_W) * (SWA_HD ** -0.5)).astype(BF16)
    skv = proj(OFF_SK, 2 * SWA_KV_W)
    kext_scr[SWA_WINDOW + r0:SWA_WINDOW + r0 + tm, :] = skv[:, :SWA_KV_W]
    vext_scr[SWA_WINDOW + r0:SWA_WINDOW + r0 + tm, :] = skv[:, SWA_KV_W:]
    for c in range(1, len(chunks)):
        retention_chunk(c)
    for hd in heads:
        state_scr[hd] = state[hd]

    lane = lax.broadcasted_iota(jnp.int32, (1, LANES), 1)
    low_half = lane < SWA_HD
    from_prev = (lax.broadcasted_iota(jnp.int32, (SWA_WINDOW, SWA_WINDOW), 1)
                 > lax.broadcasted_iota(jnp.int32, (SWA_WINDOW, SWA_WINDOW), 0))

    def lane_halves(a, kh):
        own = jnp.where(low_half if kh == 0 else jnp.logical_not(low_half), a, 0.0)
        other = pltpu.roll(own, SWA_HD, 1)
        return (own, other) if kh == 0 else (other, own)

    def swa_scores(j, kh):
        rows = slice(j * SWA_WINDOW, (j + 1) * SWA_WINDOW)
        k_lo, k_hi = lane_halves(kext_scr[r0 + j * SWA_WINDOW:r0 + (j + 2) * SWA_WINDOW, :], kh)
        kcat = jnp.concatenate([k_lo, k_hi], axis=0).astype(BF16)
        pair0 = kh * PAIRS_PER_KV
        qs = jnp.concatenate(
            [sq[rows, (pair0 + p) * LANES:(pair0 + p + 1) * LANES] for p in range(PAIRS_PER_KV)], axis=0)
        return _dot_nt(qs, kcat)

    def swa_output(j, kh, scores_jk):
        rows = slice(j * SWA_WINDOW, (j + 1) * SWA_WINDOW)
        v_lo, v_hi = lane_halves(vext_scr[r0 + j * SWA_WINDOW:r0 + (j + 2) * SWA_WINDOW, :], kh)
        vcat = jnp.concatenate([v_lo, v_hi], axis=0).astype(BF16)
        pair0 = kh * PAIRS_PER_KV
        probs = []
        inv = []
        for p in range(PAIRS_PER_KV):
            row_p = []
            inv_p = []
            for e in range(HEADS_PER_LANE_TILE):
                sb = scores_jk[p * SWA_WINDOW:(p + 1) * SWA_WINDOW, e * 2 * SWA_WINDOW:(e + 1) * 2 * SWA_WINDOW]
                s_prev, s_own = sb[:, :SWA_WINDOW], sb[:, SWA_WINDOW:]
                if s == 0 and j == 0:
                    s_prev = jnp.where(t != 0, s_prev, NEG_INF)
                dense = jnp.where(from_prev, s_prev, s_own)
                sink = sinks_ref[layer, (pair0 + p) * HEADS_PER_LANE_TILE + e]
                m = jnp.maximum(jnp.max(dense, axis=-1, keepdims=True), sink)
                pe = jnp.exp(dense - m)
                den = jnp.sum(pe, axis=-1, keepdims=True) + jnp.exp(sink - m)
                row_p.append(jnp.where(from_prev, pe, 0.0).astype(BF16))
                row_p.append(jnp.where(from_prev, 0.0, pe).astype(BF16))
                inv_p.append(1.0 / den)
            probs.append(jnp.concatenate(row_p, axis=1))
            inv.append(inv_p)
        o = _dot(jnp.concatenate(probs, axis=0), vcat)
        for p in range(PAIRS_PER_KV):
            scale = jnp.where(low_half, inv[p][0], inv[p][1])
            op = o[p * SWA_WINDOW:(p + 1) * SWA_WINDOW] * scale
            swa_scr[in_block(rows), (pair0 + p) * LANES:(pair0 + p + 1) * LANES] = op.astype(BF16)

    m_cols = [slice(hd * MEM_HD, (hd + 1) * MEM_HD) for hd in range(MEM_HEADS)]

    def mem_output(cols, s):
        m = jnp.max(s, axis=-1, keepdims=True)
        pe = jnp.exp(s - m)
        den = jnp.sum(pe, axis=-1, keepdims=True)
        o = _dot(pe.astype(BF16), mv_ref[:, cols]) * (1.0 / den)
        mo_scr[tile, cols] = o.astype(BF16)

    items = [(j, kh) for j in range(n_blk) for kh in range(SWA_KV_HEADS)]
    late = {}

    def mem_queries():
        mq = (proj(OFF_MQ, MEM_Q_W) * (MEM_HD ** -0.5)).astype(BF16)
        late["mem_s"] = [_dot_nt(mq[:, cols], mk_ref[:, cols]) for cols in m_cols]

    fillers = [
        mem_queries,
        lambda: late.update(up_ret=_dot(ret_scr[tile, :], wr_ref[...])),
        lambda: late.update(gate_ret=jax.nn.sigmoid(proj(OFF_GL, D_MODEL))),
        lambda: late.update(gate_swa=jax.nn.sigmoid(proj(OFF_GL + D_MODEL, D_MODEL))),
        lambda: late.update(gate_mem=jax.nn.sigmoid(proj(OFF_GL + 2 * D_MODEL, D_MODEL))),
    ]

    def fill():
        if fillers:
            fillers.pop(0)()

    fill()
    swa_s = {0: swa_scores(*items[0])}
    for i, (j, kh) in enumerate(items):
        if i + 1 < len(items):
            swa_s[i + 1] = swa_scores(*items[i + 1])
        fill()
        swa_output(j, kh, swa_s.pop(i))
    while fillers:
        fill()
    for cols, sc in zip(m_cols, late["mem_s"]):
        mem_output(cols, sc)

    merged = late["gate_ret"] * late["up_ret"]
    merged += late["gate_swa"] * _dot(swa_scr[tile, :], ws_ref[...])
    merged += late["gate_mem"] * _dot(mo_scr[tile, :], wm_ref[...])
    o_ref[tile, :] = x + _dot(merged.astype(BF16), wo_ref[...])


def _mix(x, batch, seq, l, cos, sin, g, w_in, decay, xi, zeta, gn, sinks, mk, mv, wr, ws, wm, wo, chunk_decay):
    n_tok, d = x.shape
    tm = MIX_TILE * MIX_TILES_PER_STEP
    steps = seq // tm
    tok = pl.BlockSpec((tm, d), lambda b, t: (b * steps + t, 0))
    pos = pl.BlockSpec((tm, RET_DK), lambda b, t: (t, 0))
    memb = pl.BlockSpec((None, MEM_LEN, MEM_Q_W), lambda b, t: (b, 0, 0))
    return pl.pallas_call(
        functools.partial(_mix_kernel, layer=l, chunk_decay=chunk_decay),
        grid=(batch, steps),
        in_specs=[pl.BlockSpec(memory_space=pltpu.SMEM), tok, pos, pos, _layer(g, l), _HBM,
                  _resident(decay.shape), _resident(xi.shape), _resident(zeta.shape), _layer(gn, l), memb, memb,
                  _HBM, _HBM, _HBM, _HBM],
        out_specs=tok,
        out_shape=jax.ShapeDtypeStruct((n_tok, d), F32),
        scratch_shapes=[pltpu.VMEM(w_in.shape[1:], BF16)] + [pltpu.VMEM((d, d), BF16)] * 4 + [
                        _stage(d // (2 * WEIGHT_CHUNKS), w_in.shape[2]), _stage(d // WEIGHT_CHUNKS, d),
                        pltpu.SemaphoreType.DMA((2,)),
                        pltpu.VMEM((RET_HEADS, RET_DK, RET_DV), F32),
                        pltpu.VMEM((tm + SWA_WINDOW, SWA_KV_W), F32),
                        pltpu.VMEM((tm + SWA_WINDOW, SWA_KV_W), F32),
                        pltpu.VMEM((tm, RET_V_W), BF16),
                        pltpu.VMEM((tm, SWA_Q_W), BF16),
                        pltpu.VMEM((tm, MEM_Q_W), BF16)],
        compiler_params=pltpu.CompilerParams(dimension_semantics=("arbitrary", "arbitrary"),
                                             vmem_limit_bytes=VMEM_LIMIT_BYTES),
        name="mix",
    )(sinks, x, cos, sin, g, w_in, decay, xi, zeta, gn, mk, mv, wr, ws, wm, wo)


def _position_tables(seq):
    half = RET_DK // 2
    inv = ROPE_BASE ** (-jnp.arange(half, dtype=F32) / half)
    ang = jnp.arange(seq, dtype=F32)[:, None] * inv[None, :]
    cos, sin = jnp.cos(ang), jnp.sin(ang)
    cos_full = jnp.concatenate([cos, cos], axis=-1)
    sin_full = jnp.concatenate([-sin, sin], axis=-1)
    log_g = jnp.log1p(-jnp.power(2.0, -5.0 - jnp.arange(RET_HEADS, dtype=F32)))
    i = jnp.arange(RET_CHUNK, dtype=F32)
    diff = i[:, None] - i[None, :]
    decay = jnp.where(diff[None] >= 0, jnp.exp(jnp.maximum(diff, 0.0)[None] * log_g[:, None, None]), 0.0)
    zeta = jnp.exp((RET_CHUNK - 1 - i)[:, None] * log_g[None, :])
    xi = jnp.exp((i + 1)[:, None] * log_g[None, :])
    widen = lambda a: jnp.repeat(a, RET_DK, axis=1)
    return cos_full, sin_full, decay, widen(xi), widen(zeta)


def kernel(x, mem, norm_ffn1, ffn1_w1, ffn1_w3, ffn1_w2, norm_mix, w_in, ret_gn, swa_sinks, norm_mem, w_mem_kv,
           w_up_ret, w_up_swa, w_up_mem, w_o, norm_ffn2, ffn2_w1, ffn2_w3, ffn2_w2, norm_final):
    batch, seq, d = x.shape
    assert d == D_MODEL and seq % (MIX_TILE * MIX_TILES_PER_STEP) == 0
    assert (batch * seq) % (FFN_TILE * FFN_TILES_PER_STEP) == 0
    assert MIX_TILE % RET_CHUNK == 0 and MIX_TILE % SWA_WINDOW == 0
    cos, sin, decay, xi, zeta = _position_tables(seq)
    chunk_decay = tuple(math.exp(RET_CHUNK * math.log1p(-(2.0 ** (-5.0 - hd)))) for hd in range(RET_HEADS))
    row = lambda g: g.reshape(DEPTH, 1, g.shape[-1])
    ffn1 = (row(norm_ffn1), ffn1_w1, ffn1_w3, ffn1_w2)
    ffn2 = (row(norm_ffn2), ffn2_w1, ffn2_w3, ffn2_w2)
    mix_w = (w_up_ret, w_up_swa, w_up_mem, w_o)
    w_kv_b = w_mem_kv.astype(BF16)
    xf = x.reshape(batch * seq, d)
    for l in range(DEPTH):
        xf = _ffn(xf, *ffn1, l, norm_final, False)
        mk, mv = _memkv(mem, row(norm_mem), w_kv_b, l)
        xf = _mix(xf, batch, seq, l, cos, sin, row(norm_mix), w_in, decay, xi, zeta, row(ret_gn), swa_sinks,
                  mk, mv, *mix_w, chunk_decay)
        xf = _ffn(xf, *ffn2, l, norm_final, l == DEPTH - 1)
    return xf.reshape(batch, seq, d)
```

```python
import functools
import math

import jax
import jax.numpy as jnp
from jax import lax
from jax.experimental import pallas as pl
from jax.experimental.pallas import tpu as pltpu

F32 = jnp.float32
BF16 = jnp.bfloat16

D_MODEL = 1024
DEPTH = 2
MEM_LEN = 256
EPS = 1e-6
ROPE_BASE = 10000.0
RET_HEADS = 4
RET_DK = 128
RET_DV = 256
RET_CHUNK = 128
SWA_Q_HEADS = 16
SWA_KV_HEADS = 2
SWA_HD = 64
SWA_WINDOW = 128
MEM_HEADS = 4
MEM_HD = 256
D_FF = 2816
NEG_INF = -1e30

RET_QK_W = RET_HEADS * RET_DK
RET_V_W = RET_HEADS * RET_DV
SWA_Q_W = SWA_Q_HEADS * SWA_HD
SWA_KV_W = SWA_KV_HEADS * SWA_HD
MEM_Q_W = MEM_HEADS * MEM_HD
OFF_RQ = 0
OFF_RK = OFF_RQ + RET_QK_W
OFF_RV = OFF_RK + RET_QK_W
OFF_RG = OFF_RV + RET_V_W
OFF_SQ = OFF_RG + RET_V_W
OFF_SK = OFF_SQ + SWA_Q_W
OFF_SV = OFF_SK + SWA_KV_W
OFF_MQ = OFF_SV + SWA_KV_W
OFF_GL = OFF_MQ + MEM_Q_W

LANES = 128
MXU_WIDTH = 256
HEADS_PER_LANE_TILE = LANES // SWA_HD
SWA_GROUP = SWA_Q_HEADS // SWA_KV_HEADS
PAIRS_PER_KV = SWA_GROUP // HEADS_PER_LANE_TILE

FFN_TILE = 512
FFN_TILES_PER_STEP = 2
MIX_TILE = 256
MIX_TILES_PER_STEP = 2
VMEM_LIMIT_BYTES = 56 * 1024 * 1024
WEIGHT_CHUNKS = 8
STAGE_SLOTS = 4


def _inv_rms(x):
    return lax.rsqrt(jnp.mean(x * x, axis=-1, keepdims=True) + EPS)


def _rmsnorm(x, g):
    return x * _inv_rms(x) * g


def _dot(a, b):
    return jnp.dot(a, b, preferred_element_type=F32)


def _dot_nt(a, b):
    return lax.dot_general(a, b, (((1,), (1,)), ((), ())), preferred_element_type=F32)


def _dot_tn(a, b):
    return lax.dot_general(a, b, (((0,), (0,)), ((), ())), preferred_element_type=F32)


def _resident(shape):
    zeros = (0,) * len(shape)
    return pl.BlockSpec(shape, lambda *_: zeros, pipeline_mode=pl.Buffered(1))


def _layer(stacked, l):
    zeros = (0,) * (stacked.ndim - 1)
    return pl.BlockSpec((None,) + stacked.shape[1:], lambda *_: (l,) + zeros, pipeline_mode=pl.Buffered(1))


_HBM = pl.BlockSpec(memory_space=pl.ANY)


def _stage(rows, cols):
    return [pltpu.VMEM((STAGE_SLOTS, rows, cols), F32), pltpu.SemaphoreType.DMA((STAGE_SLOTS,))]


def _load_as_bf16(jobs):
    plan = []
    slot = 0
    for k, (src, dst, stage, sem) in enumerate(jobs):
        rows = stage.shape[1]
        assert src.shape[0] % rows == 0 and src.shape[1] == stage.shape[2]
        if k and stage is not jobs[k - 1][2]:
            slot = 0
        for row0 in range(0, src.shape[0], rows):
            plan.append((src, dst, row0, rows, stage, sem, slot))
            slot = (slot + 1) % STAGE_SLOTS

    def copy(i):
        src, _, row0, rows, stage, sem, slot = plan[i]
        return pltpu.make_async_copy(src.at[pl.ds(row0, rows), :], stage.at[slot], sem.at[slot])

    ahead = STAGE_SLOTS - 1
    for i in range(min(ahead, len(plan))):
        copy(i).start()
    for i, (_, dst, row0, rows, stage, _, slot) in enumerate(plan):
        if i + ahead < len(plan):
            copy(i + ahead).start()
        copy(i).wait()
        dst[row0:row0 + rows, :] = stage[slot].astype(BF16)


def _ffn_kernel(x_ref, g_ref, w1_hbm, w3_hbm, w2_hbm, gf_ref, o_ref,
                w1_ref, w3_ref, w2_ref, stage_in, sem_in, stage_out, sem_out, *, layer, final_norm):
    @pl.when(pl.program_id(0) == 0)
    def _():
        _load_as_bf16([(w1_hbm.at[layer], w1_ref, stage_in, sem_in), (w3_hbm.at[layer], w3_ref, stage_in, sem_in),
                       (w2_hbm.at[layer], w2_ref, stage_out, sem_out)])

    for s in range(x_ref.shape[0] // FFN_TILE):
        rows = slice(s * FFN_TILE, (s + 1) * FFN_TILE)
        x = x_ref[rows, :]
        h = (x * g_ref[...]).astype(BF16)
        r = _inv_rms(x)
        u = []
        for c in range(0, w1_ref.shape[1], MXU_WIDTH):
            a = _dot(h, w1_ref[:, c:c + MXU_WIDTH]) * r
            b = _dot(h, w3_ref[:, c:c + MXU_WIDTH]) * r
            u.append((jax.nn.silu(a) * b).astype(BF16))
        u = jnp.concatenate(u, axis=1)
        out = x + 0.5 * _dot(u, w2_ref[...])
        if final_norm:
            out = _rmsnorm(out, gf_ref[...])
        o_ref[rows, :] = out


def _ffn(x, g, w1, w3, w2, l, g_final, final_norm):
    n_tok, d = x.shape
    block_rows = FFN_TILE * FFN_TILES_PER_STEP
    block = pl.BlockSpec((block_rows, d), lambda i: (i, 0))
    f = w1.shape[2]
    return pl.pallas_call(
        functools.partial(_ffn_kernel, layer=l, final_norm=final_norm),
        grid=(n_tok // block_rows,),
        in_specs=[block, _layer(g, l), _HBM, _HBM, _HBM, _resident((1, d))],
        out_specs=block,
        out_shape=jax.ShapeDtypeStruct((n_tok, d), F32),
        scratch_shapes=[pltpu.VMEM((d, f), BF16), pltpu.VMEM((d, f), BF16), pltpu.VMEM((f, d), BF16),
                        *_stage(d // WEIGHT_CHUNKS, f), *_stage(f // WEIGHT_CHUNKS, d)],
        compiler_params=pltpu.CompilerParams(dimension_semantics=("arbitrary",),
                                             vmem_limit_bytes=VMEM_LIMIT_BYTES),
        name="ffn_final" if final_norm else "ffn",
    )(x, g, w1, w3, w2, g_final.reshape(1, d))


def _memkv_kernel(mem_ref, g_ref, w_ref, k_ref, v_ref):
    h = _rmsnorm(mem_ref[...], g_ref[...]).astype(BF16)
    kv = _dot(h, w_ref[...])
    k_ref[...] = kv[:, :MEM_Q_W].astype(BF16)
    v_ref[...] = kv[:, MEM_Q_W:].astype(BF16)


def _memkv(mem, g, w, l):
    b, m, d = mem.shape
    out = jax.ShapeDtypeStruct((b, m, MEM_Q_W), BF16)
    return pl.pallas_call(
        _memkv_kernel,
        grid=(b,),
        in_specs=[pl.BlockSpec((None, m, d), lambda i: (i, 0, 0)), _layer(g, l), _layer(w, l)],
        out_specs=[pl.BlockSpec((None, m, MEM_Q_W), lambda i: (i, 0, 0))] * 2,
        out_shape=[out, out],
        compiler_params=pltpu.CompilerParams(dimension_semantics=("arbitrary",),
                                             vmem_limit_bytes=VMEM_LIMIT_BYTES),
        name="mem_kv",
    )(mem, g, w)


def _mix_kernel(sinks_ref, x_ref, cos_ref, sin_ref, g_ref, win_hbm, decay_ref, xi_ref, zeta_ref, gn_ref,
                mk_ref, mv_ref, wr_hbm, ws_hbm, wm_hbm, wo_hbm, o_ref,
                win_ref, wr_ref, ws_ref, wm_ref, wo_ref, stage_in, sem_in, stage_up, sem_up,
                state_scr, kext_scr, vext_scr, ret_scr, swa_scr, mo_scr, *, layer, chunk_decay):
    block_rows = x_ref.shape[0]

    @pl.when(jnp.logical_and(pl.program_id(0) == 0, pl.program_id(1) == 0))
    def _():
        ups = ((wr_hbm, wr_ref), (ws_hbm, ws_ref), (wm_hbm, wm_ref), (wo_hbm, wo_ref))
        _load_as_bf16([(win_hbm.at[layer], win_ref, stage_in, sem_in)]
                      + [(src.at[layer], dst, stage_up, sem_up) for src, dst in ups])

    @pl.when(pl.program_id(1) == 0)
    def _():
        state_scr[...] = jnp.zeros_like(state_scr)
        kext_scr[0:SWA_WINDOW, :] = jnp.zeros((SWA_WINDOW, SWA_KV_W), F32)
        vext_scr[0:SWA_WINDOW, :] = jnp.zeros((SWA_WINDOW, SWA_KV_W), F32)

    for s in range(block_rows // MIX_TILE):
        _mix_tile(s, sinks_ref, x_ref, cos_ref, sin_ref, g_ref, win_ref, decay_ref, xi_ref, zeta_ref, gn_ref,
                  mk_ref, mv_ref, wr_ref, ws_ref, wm_ref, wo_ref, o_ref,
                  state_scr, kext_scr, vext_scr, ret_scr, swa_scr, mo_scr, layer=layer, chunk_decay=chunk_decay)
    kext_scr[0:SWA_WINDOW, :] = kext_scr[block_rows:block_rows + SWA_WINDOW, :]
    vext_scr[0:SWA_WINDOW, :] = vext_scr[block_rows:block_rows + SWA_WINDOW, :]


def _mix_tile(s, sinks_ref, x_ref, cos_ref, sin_ref, g_ref, win_ref, decay_ref, xi_ref, zeta_ref, gn_ref,
              mk_ref, mv_ref, wr_ref, ws_ref, wm_ref, wo_ref, o_ref,
              state_scr, kext_scr, vext_scr, ret_scr, swa_scr, mo_scr, *, layer, chunk_decay):
    t = pl.program_id(1)
    tm = MIX_TILE
    r0 = s * tm
    tile = slice(r0, r0 + tm)
    in_block = lambda rows: slice(r0 + rows.start, r0 + rows.stop)
    n_blk = tm // SWA_WINDOW

    x = x_ref[tile, :]
    h = _rmsnorm(x, g_ref[...]).astype(BF16)

    def proj(off, width):
        return _dot(h, win_ref[:, off:off + width])


    cos = cos_ref[tile, :]
    sin = sin_ref[tile, :]
    heads = range(RET_HEADS)
    chunks = [slice(c * RET_CHUNK, (c + 1) * RET_CHUNK) for c in range(tm // RET_CHUNK)]
    qk_cols = [slice(hd * RET_DK, (hd + 1) * RET_DK) for hd in heads]
    v_cols = [slice(hd * RET_DV, (hd + 1) * RET_DV) for hd in heads]

    def rotate(v):
        return v * cos + pltpu.roll(v, RET_DK // 2, 1) * sin

    rq = proj(OFF_RQ, RET_QK_W)
    rk = proj(OFF_RK, RET_QK_W)
    q = [rotate(rq[:, qk_cols[hd]]) for hd in heads]
    k = [rotate(rk[:, qk_cols[hd]]) * (RET_DK ** -0.5) for hd in heads]
    v = proj(OFF_RV, RET_V_W).astype(BF16)
    scores = [[_dot_nt(q[hd][rows].astype(BF16), k[hd][rows].astype(BF16)) for rows in chunks] for hd in heads]
    chunk_state = [[_dot_tn((k[hd][rows] * zeta_ref[:, qk_cols[hd]]).astype(BF16), v[rows, v_cols[hd]])
                    for rows in chunks] for hd in heads]
    gate = jax.nn.silu(proj(OFF_RG, RET_V_W))
    state = [state_scr[hd] for hd in heads]

    def retention_chunk(c):
        rows = chunks[c]
        for hd in heads:
            s = scores[hd][c] * decay_ref[hd]
            lhs = jnp.concatenate([s.astype(BF16), (q[hd][rows] * xi_ref[:, qk_cols[hd]]).astype(BF16)], axis=1)
            rhs = jnp.concatenate([v[rows, v_cols[hd]], state[hd].astype(BF16)], axis=0)
            o = _rmsnorm(_dot(lhs, rhs), gn_ref[:, v_cols[hd]])
            ret_scr[in_block(rows), v_cols[hd]] = (o * gate[rows, v_cols[hd]]).astype(BF16)
            state[hd] = state[hd] * chunk_decay[hd] + chunk_state[hd][c]

    retention_chunk(0)
    sq = (proj(OFF_SQ, SWA_Q_W) * (SWA_HD ** -0.5)).astype(BF16)
    skv = proj(OFF_SK, 2 * SWA_KV_W)
    kext_scr[SWA_WINDOW + r0:SWA_WINDOW + r0 + tm, :] = skv[:, :SWA_KV_W]
    vext_scr[SWA_WINDOW + r0:SWA_WINDOW + r0 + tm, :] = skv[:, SWA_KV_W:]
    for c in range(1, len(chunks)):
        retention_chunk(c)
    for hd in heads:
        state_scr[hd] = state[hd]

    lane = lax.broadcasted_iota(jnp.int32, (1, LANES), 1)
    low_half = lane < SWA_HD
    from_prev = (lax.broadcasted_iota(jnp.int32, (SWA_WINDOW, SWA_WINDOW), 1)
                 > lax.broadcasted_iota(jnp.int32, (SWA_WINDOW, SWA_WINDOW), 0))

    def lane_halves(a, kh):
        own = jnp.where(low_half if kh == 0 else jnp.logical_not(low_half), a, 0.0)
        other = pltpu.roll(own, SWA_HD, 1)
        return (own, other) if kh == 0 else (other, own)

    def swa_scores(j, kh):
        rows = slice(j * SWA_WINDOW, (j + 1) * SWA_WINDOW)
        k_lo, k_hi = lane_halves(kext_scr[r0 + j * SWA_WINDOW:r0 + (j + 2) * SWA_WINDOW, :], kh)
        kcat = jnp.concatenate([k_lo, k_hi], axis=0).astype(BF16)
        pair0 = kh * PAIRS_PER_KV
        qs = jnp.concatenate(
            [sq[rows, (pair0 + p) * LANES:(pair0 + p + 1) * LANES] for p in range(PAIRS_PER_KV)], axis=0)
        return _dot_nt(qs, kcat)

    def swa_output(j, kh, scores_jk):
        rows = slice(j * SWA_WINDOW, (j + 1) * SWA_WINDOW)
        v_lo, v_hi = lane_halves(vext_scr[r0 + j * SWA_WINDOW:r0 + (j + 2) * SWA_WINDOW, :], kh)
        vcat = jnp.concatenate([v_lo, v_hi], axis=0).astype(BF16)
        pair0 = kh * PAIRS_PER_KV
        probs = []
        inv = []
        for p in range(PAIRS_PER_KV):
            row_p = []
            inv_p = []
            for e in range(HEADS_PER_LANE_TILE):
                sb = scores_jk[p * SWA_WINDOW:(p + 1) * SWA_WINDOW, e * 2 * SWA_WINDOW:(e + 1) * 2 * SWA_WINDOW]
                s_prev, s_own = sb[:, :SWA_WINDOW], sb[:, SWA_WINDOW:]
                if s == 0 and j == 0:
                    s_prev = jnp.where(t != 0, s_prev, NEG_INF)
                dense = jnp.where(from_prev, s_prev, s_own)
                sink = sinks_ref[layer, (pair0 + p) * HEADS_PER_LANE_TILE + e]
                m = jnp.maximum(jnp.max(dense, axis=-1, keepdims=True), sink)
                pe = jnp.exp(dense - m)
                den = jnp.sum(pe, axis=-1, keepdims=True) + jnp.exp(sink - m)
                row_p.append(jnp.where(from_prev, pe, 0.0).astype(BF16))
                row_p.append(jnp.where(from_prev, 0.0, pe).astype(BF16))
                inv_p.append(1.0 / den)
            probs.append(jnp.concatenate(row_p, axis=1))
            inv.append(inv_p)
        o = _dot(jnp.concatenate(probs, axis=0), vcat)
        for p in range(PAIRS_PER_KV):
            scale = jnp.where(low_half, inv[p][0], inv[p][1])
            op = o[p * SWA_WINDOW:(p + 1) * SWA_WINDOW] * scale
            swa_scr[in_block(rows), (pair0 + p) * LANES:(pair0 + p + 1) * LANES] = op.astype(BF16)

    m_cols = [slice(hd * MEM_HD, (hd + 1) * MEM_HD) for hd in range(MEM_HEADS)]

    def mem_output(cols, s):
        m = jnp.max(s, axis=-1, keepdims=True)
        pe = jnp.exp(s - m)
        den = jnp.sum(pe, axis=-1, keepdims=True)
        o = _dot(pe.astype(BF16), mv_ref[:, cols]) * (1.0 / den)
        mo_scr[tile, cols] = o.astype(BF16)

    items = [(j, kh) for j in range(n_blk) for kh in range(SWA_KV_HEADS)]
    late = {}

    def mem_queries():
        mq = (proj(OFF_MQ, MEM_Q_W) * (MEM_HD ** -0.5)).astype(BF16)
        late["mem_s"] = [_dot_nt(mq[:, cols], mk_ref[:, cols]) for cols in m_cols]

    fillers = [
        mem_queries,
        lambda: late.update(up_ret=_dot(ret_scr[tile, :], wr_ref[...])),
        lambda: late.update(gate_ret=jax.nn.sigmoid(proj(OFF_GL, D_MODEL))),
        lambda: late.update(gate_swa=jax.nn.sigmoid(proj(OFF_GL + D_MODEL, D_MODEL))),
        lambda: late.update(gate_mem=jax.nn.sigmoid(proj(OFF_GL + 2 * D_MODEL, D_MODEL))),
    ]

    def fill():
        if fillers:
            fillers.pop(0)()

    fill()
    swa_s = {0: swa_scores(*items[0])}
    for i, (j, kh) in enumerate(items):
        if i + 1 < len(items):
            swa_s[i + 1] = swa_scores(*items[i + 1])
        fill()
        swa_output(j, kh, swa_s.pop(i))
    while fillers:
        fill()
    for cols, sc in zip(m_cols, late["mem_s"]):
        mem_output(cols, sc)

    merged = late["gate_ret"] * late["up_ret"]
    merged += late["gate_swa"] * _dot(swa_scr[tile, :], ws_ref[...])
    merged += late["gate_mem"] * _dot(mo_scr[tile, :], wm_ref[...])
    o_ref[tile, :] = x + _dot(merged.astype(BF16), wo_ref[...])


def _mix(x, batch, seq, l, cos, sin, g, w_in, decay, xi, zeta, gn, sinks, mk, mv, wr, ws, wm, wo, chunk_decay):
    n_tok, d = x.shape
    tm = MIX_TILE * MIX_TILES_PER_STEP
    steps = seq // tm
    tok = pl.BlockSpec((tm, d), lambda b, t: (b * steps + t, 0))
    pos = pl.BlockSpec((tm, RET_DK), lambda b, t: (t, 0))
    memb = pl.BlockSpec((None, MEM_LEN, MEM_Q_W), lambda b, t: (b, 0, 0))
    return pl.pallas_call(
        functools.partial(_mix_kernel, layer=l, chunk_decay=chunk_decay),
        grid=(batch, steps),
        in_specs=[pl.BlockSpec(memory_space=pltpu.SMEM), tok, pos, pos, _layer(g, l), _HBM,
                  _resident(decay.shape), _resident(xi.shape), _resident(zeta.shape), _layer(gn, l), memb, memb,
                  _HBM, _HBM, _HBM, _HBM],
        out_specs=tok,
        out_shape=jax.ShapeDtypeStruct((n_tok, d), F32),
        scratch_shapes=[pltpu.VMEM(w_in.shape[1:], BF16)] + [pltpu.VMEM((d, d), BF16)] * 4 + [
                        *_stage(d // (4 * WEIGHT_CHUNKS), w_in.shape[2]), *_stage(d // WEIGHT_CHUNKS * 2, d),
                        pltpu.VMEM((RET_HEADS, RET_DK, RET_DV), F32),
                        pltpu.VMEM((tm + SWA_WINDOW, SWA_KV_W), F32),
                        pltpu.VMEM((tm + SWA_WINDOW, SWA_KV_W), F32),
                        pltpu.VMEM((tm, RET_V_W), BF16),
                        pltpu.VMEM((tm, SWA_Q_W), BF16),
                        pltpu.VMEM((tm, MEM_Q_W), BF16)],
        compiler_params=pltpu.CompilerParams(dimension_semantics=("arbitrary", "arbitrary"),
                                             vmem_limit_bytes=VMEM_LIMIT_BYTES),
        name="mix",
    )(sinks, x, cos, sin, g, w_in, decay, xi, zeta, gn, mk, mv, wr, ws, wm, wo)


def _position_tables(seq):
    half = RET_DK // 2
    inv = ROPE_BASE ** (-jnp.arange(half, dtype=F32) / half)
    ang = jnp.arange(seq, dtype=F32)[:, None] * inv[None, :]
    cos, sin = jnp.cos(ang), jnp.sin(ang)
    cos_full = jnp.concatenate([cos, cos], axis=-1)
    sin_full = jnp.concatenate([-sin, sin], axis=-1)
    log_g = jnp.log1p(-jnp.power(2.0, -5.0 - jnp.arange(RET_HEADS, dtype=F32)))
    i = jnp.arange(RET_CHUNK, dtype=F32)
    diff = i[:, None] - i[None, :]
    decay = jnp.where(diff[None] >= 0, jnp.exp(jnp.maximum(diff, 0.0)[None] * log_g[:, None, None]), 0.0)
    zeta = jnp.exp((RET_CHUNK - 1 - i)[:, None] * log_g[None, :])
    xi = jnp.exp((i + 1)[:, None] * log_g[None, :])
    widen = lambda a: jnp.repeat(a, RET_DK, axis=1)
    return cos_full, sin_full, decay, widen(xi), widen(zeta)


def kernel(x, mem, norm_ffn1, ffn1_w1, ffn1_w3, ffn1_w2, norm_mix, w_in, ret_gn, swa_sinks, norm_mem, w_mem_kv,
           w_up_ret, w_up_swa, w_up_mem, w_o, norm_ffn2, ffn2_w1, ffn2_w3, ffn2_w2, norm_final):
    batch, seq, d = x.shape
    assert d == D_MODEL and seq % (MIX_TILE * MIX_TILES_PER_STEP) == 0
    assert (batch * seq) % (FFN_TILE * FFN_TILES_PER_STEP) == 0
    assert MIX_TILE % RET_CHUNK == 0 and MIX_TILE % SWA_WINDOW == 0
    cos, sin, decay, xi, zeta = _position_tables(seq)
    chunk_decay = tuple(math.exp(RET_CHUNK * math.log1p(-(2.0 ** (-5.0 - hd)))) for hd in range(RET_HEADS))
    row = lambda g: g.reshape(DEPTH, 1, g.shape[-1])
    ffn1 = (row(norm_ffn1), ffn1_w1, ffn1_w3, ffn1_w2)
    ffn2 = (row(norm_ffn2), ffn2_w1, ffn2_w3, ffn2_w2)
    mix_w = (w_up_ret, w_up_swa, w_up_mem, w_o)
    w_kv_b = w_mem_kv.astype(BF16)
    xf = x.reshape(batch * seq, d)
    for l in range(DEPTH):
        xf = _ffn(xf, *ffn1, l, norm_final, False)
        mk, mv = _memkv(mem, row(norm_mem), w_kv_b, l)
        xf = _mix(xf, batch, seq, l, cos, sin, row(norm_mix), w_in, decay, xi, zeta, row(ret_gn), swa_sinks,
                  mk, mv, *mix_w, chunk_decay)
        xf = _ffn(xf, *ffn2, l, norm_final, l == DEPTH - 1)
    return xf.reshape(batch, seq, d)
```

```python
import functools
import math

import jax
import jax.numpy as jnp
from jax import lax
from jax.experimental import pallas as pl
from jax.experimental.pallas import tpu as pltpu

F32 = jnp.float32
BF16 = jnp.bfloat16

D_MODEL = 1024
DEPTH = 2
MEM_LEN = 256
EPS = 1e-6
ROPE_BASE = 10000.0
RET_HEADS = 4
RET_DK = 128
RET_DV = 256
RET_CHUNK = 128
SWA_Q_HEADS = 16
SWA_KV_HEADS = 2
SWA_HD = 64
SWA_WINDOW = 128
MEM_HEADS = 4
MEM_HD = 256
D_FF = 2816
NEG_INF = -1e30

RET_QK_W = RET_HEADS * RET_DK
RET_V_W = RET_HEADS * RET_DV
SWA_Q_W = SWA_Q_HEADS * SWA_HD
SWA_KV_W = SWA_KV_HEADS * SWA_HD
MEM_Q_W = MEM_HEADS * MEM_HD
OFF_RQ = 0
OFF_RK = OFF_RQ + RET_QK_W
OFF_RV = OFF_RK + RET_QK_W
OFF_RG = OFF_RV + RET_V_W
OFF_SQ = OFF_RG + RET_V_W
OFF_SK = OFF_SQ + SWA_Q_W
OFF_SV = OFF_SK + SWA_KV_W
OFF_MQ = OFF_SV + SWA_KV_W
OFF_GL = OFF_MQ + MEM_Q_W

LANES = 128
MXU_WIDTH = 256
HEADS_PER_LANE_TILE = LANES // SWA_HD
SWA_GROUP = SWA_Q_HEADS // SWA_KV_HEADS
PAIRS_PER_KV = SWA_GROUP // HEADS_PER_LANE_TILE

FFN_TILE = 512
FFN_TILES_PER_STEP = 2
MIX_TILE = 256
MIX_TILES_PER_STEP = 2
MEMKV_TILE = 1024
VMEM_LIMIT_BYTES = 56 * 1024 * 1024
WEIGHT_CHUNKS = 8
STAGE_SLOTS = 4


def _inv_rms(x):
    return lax.rsqrt(jnp.mean(x * x, axis=-1, keepdims=True) + EPS)


def _rmsnorm(x, g):
    return x * _inv_rms(x) * g


def _dot(a, b):
    return jnp.dot(a, b, preferred_element_type=F32)


def _dot_nt(a, b):
    return lax.dot_general(a, b, (((1,), (1,)), ((), ())), preferred_element_type=F32)


def _dot_tn(a, b):
    return lax.dot_general(a, b, (((0,), (0,)), ((), ())), preferred_element_type=F32)


def _resident(shape):
    zeros = (0,) * len(shape)
    return pl.BlockSpec(shape, lambda *_: zeros, pipeline_mode=pl.Buffered(1))


def _layer(stacked, l):
    zeros = (0,) * (stacked.ndim - 1)
    return pl.BlockSpec((None,) + stacked.shape[1:], lambda *_: (l,) + zeros, pipeline_mode=pl.Buffered(1))


_HBM = pl.BlockSpec(memory_space=pl.ANY)


def _stage(rows, cols):
    return [pltpu.VMEM((STAGE_SLOTS, rows, cols), F32), pltpu.SemaphoreType.DMA((STAGE_SLOTS,))]


def _load_as_bf16(jobs):
    plan = []
    slot = 0
    for k, (src, dst, stage, sem) in enumerate(jobs):
        rows = stage.shape[1]
        assert src.shape[0] % rows == 0 and src.shape[1] == stage.shape[2]
        if k and stage is not jobs[k - 1][2]:
            slot = 0
        for row0 in range(0, src.shape[0], rows):
            plan.append((src, dst, row0, rows, stage, sem, slot))
            slot = (slot + 1) % STAGE_SLOTS

    def copy(i):
        src, _, row0, rows, stage, sem, slot = plan[i]
        return pltpu.make_async_copy(src.at[pl.ds(row0, rows), :], stage.at[slot], sem.at[slot])

    ahead = STAGE_SLOTS - 1
    for i in range(min(ahead, len(plan))):
        copy(i).start()
    for i, (_, dst, row0, rows, stage, _, slot) in enumerate(plan):
        if i + ahead < len(plan):
            copy(i + ahead).start()
        copy(i).wait()
        dst[row0:row0 + rows, :] = stage[slot].astype(BF16)


def _ffn_kernel(x_ref, g_ref, w1_hbm, w3_hbm, w2_hbm, gf_ref, o_ref,
                w1_ref, w3_ref, w2_ref, stage_in, sem_in, stage_out, sem_out, *, layer, final_norm):
    @pl.when(pl.program_id(0) == 0)
    def _():
        _load_as_bf16([(w1_hbm.at[layer], w1_ref, stage_in, sem_in), (w3_hbm.at[layer], w3_ref, stage_in, sem_in),
                       (w2_hbm.at[layer], w2_ref, stage_out, sem_out)])

    for s in range(x_ref.shape[0] // FFN_TILE):
        rows = slice(s * FFN_TILE, (s + 1) * FFN_TILE)
        x = x_ref[rows, :]
        h = (x * g_ref[...]).astype(BF16)
        r = _inv_rms(x)
        u = []
        for c in range(0, w1_ref.shape[1], MXU_WIDTH):
            a = _dot(h, w1_ref[:, c:c + MXU_WIDTH]) * r
            b = _dot(h, w3_ref[:, c:c + MXU_WIDTH]) * r
            u.append((jax.nn.silu(a) * b).astype(BF16))
        u = jnp.concatenate(u, axis=1)
        out = x + 0.5 * _dot(u, w2_ref[...])
        if final_norm:
            out = _rmsnorm(out, gf_ref[...])
        o_ref[rows, :] = out


def _ffn(x, g, w1, w3, w2, l, g_final, final_norm):
    n_tok, d = x.shape
    block_rows = FFN_TILE * FFN_TILES_PER_STEP
    block = pl.BlockSpec((block_rows, d), lambda i: (i, 0))
    f = w1.shape[2]
    return pl.pallas_call(
        functools.partial(_ffn_kernel, layer=l, final_norm=final_norm),
        grid=(n_tok // block_rows,),
        in_specs=[block, _layer(g, l), _HBM, _HBM, _HBM, _resident((1, d))],
        out_specs=block,
        out_shape=jax.ShapeDtypeStruct((n_tok, d), F32),
        scratch_shapes=[pltpu.VMEM((d, f), BF16), pltpu.VMEM((d, f), BF16), pltpu.VMEM((f, d), BF16),
                        *_stage(d // WEIGHT_CHUNKS, f), *_stage(f // WEIGHT_CHUNKS, d)],
        compiler_params=pltpu.CompilerParams(dimension_semantics=("arbitrary",),
                                             vmem_limit_bytes=VMEM_LIMIT_BYTES),
        name="ffn_final" if final_norm else "ffn",
    )(x, g, w1, w3, w2, g_final.reshape(1, d))


def _memkv_kernel(mem_ref, g_ref, w_ref, k_ref, v_ref):
    h = _rmsnorm(mem_ref[...], g_ref[...]).astype(BF16)
    kv = _dot(h, w_ref[...])
    k_ref[...] = kv[:, :MEM_Q_W].astype(BF16)
    v_ref[...] = kv[:, MEM_Q_W:].astype(BF16)


def _memkv(mem, g, w, l):
    b, m, d = mem.shape
    rows = b * m
    assert rows % MEMKV_TILE == 0
    out = jax.ShapeDtypeStruct((rows, MEM_Q_W), BF16)
    mk, mv = pl.pallas_call(
        _memkv_kernel,
        grid=(rows // MEMKV_TILE,),
        in_specs=[pl.BlockSpec((MEMKV_TILE, d), lambda i: (i, 0)), _layer(g, l), _layer(w, l)],
        out_specs=[pl.BlockSpec((MEMKV_TILE, MEM_Q_W), lambda i: (i, 0))] * 2,
        out_shape=[out, out],
        compiler_params=pltpu.CompilerParams(dimension_semantics=("arbitrary",),
                                             vmem_limit_bytes=VMEM_LIMIT_BYTES),
        name="mem_kv",
    )(mem.reshape(rows, d), g, w)
    return mk.reshape(b, m, MEM_Q_W), mv.reshape(b, m, MEM_Q_W)


def _mix_kernel(sinks_ref, x_ref, cos_ref, sin_ref, g_ref, win_hbm, decay_ref, xi_ref, zeta_ref, gn_ref,
                mk_ref, mv_ref, wr_hbm, ws_hbm, wm_hbm, wo_hbm, o_ref,
                win_ref, wr_ref, ws_ref, wm_ref, wo_ref, stage_in, sem_in, stage_up, sem_up,
                state_scr, kext_scr, vext_scr, ret_scr, swa_scr, mo_scr, *, layer, chunk_decay):
    block_rows = x_ref.shape[0]

    @pl.when(jnp.logical_and(pl.program_id(0) == 0, pl.program_id(1) == 0))
    def _():
        ups = ((wr_hbm, wr_ref), (ws_hbm, ws_ref), (wm_hbm, wm_ref), (wo_hbm, wo_ref))
        _load_as_bf16([(win_hbm.at[layer], win_ref, stage_in, sem_in)]
                      + [(src.at[layer], dst, stage_up, sem_up) for src, dst in ups])

    @pl.when(pl.program_id(1) == 0)
    def _():
        state_scr[...] = jnp.zeros_like(state_scr)
        kext_scr[0:SWA_WINDOW, :] = jnp.zeros((SWA_WINDOW, SWA_KV_W), F32)
        vext_scr[0:SWA_WINDOW, :] = jnp.zeros((SWA_WINDOW, SWA_KV_W), F32)

    for s in range(block_rows // MIX_TILE):
        _mix_tile(s, sinks_ref, x_ref, cos_ref, sin_ref, g_ref, win_ref, decay_ref, xi_ref, zeta_ref, gn_ref,
                  mk_ref, mv_ref, wr_ref, ws_ref, wm_ref, wo_ref, o_ref,
                  state_scr, kext_scr, vext_scr, ret_scr, swa_scr, mo_scr, layer=layer, chunk_decay=chunk_decay)
    kext_scr[0:SWA_WINDOW, :] = kext_scr[block_rows:block_rows + SWA_WINDOW, :]
    vext_scr[0:SWA_WINDOW, :] = vext_scr[block_rows:block_rows + SWA_WINDOW, :]


def _mix_tile(s, sinks_ref, x_ref, cos_ref, sin_ref, g_ref, win_ref, decay_ref, xi_ref, zeta_ref, gn_ref,
              mk_ref, mv_ref, wr_ref, ws_ref, wm_ref, wo_ref, o_ref,
              state_scr, kext_scr, vext_scr, ret_scr, swa_scr, mo_scr, *, layer, chunk_decay):
    t = pl.program_id(1)
    tm = MIX_TILE
    r0 = s * tm
    tile = slice(r0, r0 + tm)
    in_block = lambda rows: slice(r0 + rows.start, r0 + rows.stop)
    n_blk = tm // SWA_WINDOW

    x = x_ref[tile, :]
    h = _rmsnorm(x, g_ref[...]).astype(BF16)

    def proj(off, width):
        return _dot(h, win_ref[:, off:off + width])


    cos = cos_ref[tile, :]
    sin = sin_ref[tile, :]
    heads = range(RET_HEADS)
    chunks = [slice(c * RET_CHUNK, (c + 1) * RET_CHUNK) for c in range(tm // RET_CHUNK)]
    qk_cols = [slice(hd * RET_DK, (hd + 1) * RET_DK) for hd in heads]
    v_cols = [slice(hd * RET_DV, (hd + 1) * RET_DV) for hd in heads]

    def rotate(v):
        return v * cos + pltpu.roll(v, RET_DK // 2, 1) * sin

    rq = proj(OFF_RQ, RET_QK_W)
    rk = proj(OFF_RK, RET_QK_W)
    q = [rotate(rq[:, qk_cols[hd]]) for hd in heads]
    k = [rotate(rk[:, qk_cols[hd]]) * (RET_DK ** -0.5) for hd in heads]
    v = proj(OFF_RV, RET_V_W).astype(BF16)
    scores = [[_dot_nt(q[hd][rows].astype(BF16), k[hd][rows].astype(BF16)) for rows in chunks] for hd in heads]
    chunk_state = [[_dot_tn((k[hd][rows] * zeta_ref[:, qk_cols[hd]]).astype(BF16), v[rows, v_cols[hd]])
                    for rows in chunks] for hd in heads]
    gate = jax.nn.silu(proj(OFF_RG, RET_V_W))
    state = [state_scr[hd] for hd in heads]

    def retention_chunk(c):
        rows = chunks[c]
        for hd in heads:
            s = scores[hd][c] * decay_ref[hd]
            lhs = jnp.concatenate([s.astype(BF16), (q[hd][rows] * xi_ref[:, qk_cols[hd]]).astype(BF16)], axis=1)
            rhs = jnp.concatenate([v[rows, v_cols[hd]], state[hd].astype(BF16)], axis=0)
            o = _rmsnorm(_dot(lhs, rhs), gn_ref[:, v_cols[hd]])
            ret_scr[in_block(rows), v_cols[hd]] = (o * gate[rows, v_cols[hd]]).astype(BF16)
            state[hd] = state[hd] * chunk_decay[hd] + chunk_state[hd][c]

    retention_chunk(0)
    sq = (proj(OFF_SQ, SWA_Q_W) * (SWA_HD ** -0.5)).astype(BF16)
    skv = proj(OFF_SK, 2 * SWA_KV_W)
    kext_scr[SWA_WINDOW + r0:SWA_WINDOW + r0 + tm, :] = skv[:, :SWA_KV_W]
    vext_scr[SWA_WINDOW + r0:SWA_WINDOW + r0 + tm, :] = skv[:, SWA_KV_W:]
    for c in range(1, len(chunks)):
        retention_chunk(c)
    for hd in heads:
        state_scr[hd] = state[hd]

    lane = lax.broadcasted_iota(jnp.int32, (1, LANES), 1)
    low_half = lane < SWA_HD
    from_prev = (lax.broadcasted_iota(jnp.int32, (SWA_WINDOW, SWA_WINDOW), 1)
                 > lax.broadcasted_iota(jnp.int32, (SWA_WINDOW, SWA_WINDOW), 0))

    def lane_halves(a, kh):
        own = jnp.where(low_half if kh == 0 else jnp.logical_not(low_half), a, 0.0)
        other = pltpu.roll(own, SWA_HD, 1)
        return (own, other) if kh == 0 else (other, own)

    def swa_scores(j, kh):
        rows = slice(j * SWA_WINDOW, (j + 1) * SWA_WINDOW)
        k_lo, k_hi = lane_halves(kext_scr[r0 + j * SWA_WINDOW:r0 + (j + 2) * SWA_WINDOW, :], kh)
        kcat = jnp.concatenate([k_lo, k_hi], axis=0).astype(BF16)
        pair0 = kh * PAIRS_PER_KV
        qs = jnp.concatenate(
            [sq[rows, (pair0 + p) * LANES:(pair0 + p + 1) * LANES] for p in range(PAIRS_PER_KV)], axis=0)
        return _dot_nt(qs, kcat)

    def swa_output(j, kh, scores_jk):
        rows = slice(j * SWA_WINDOW, (j + 1) * SWA_WINDOW)
        v_lo, v_hi = lane_halves(vext_scr[r0 + j * SWA_WINDOW:r0 + (j + 2) * SWA_WINDOW, :], kh)
        vcat = jnp.concatenate([v_lo, v_hi], axis=0).astype(BF16)
        pair0 = kh * PAIRS_PER_KV
        probs = []
        inv = []
        for p in range(PAIRS_PER_KV):
            row_p = []
            inv_p = []
            for e in range(HEADS_PER_LANE_TILE):
                sb = scores_jk[p * SWA_WINDOW:(p + 1) * SWA_WINDOW, e * 2 * SWA_WINDOW:(e + 1) * 2 * SWA_WINDOW]
                s_prev, s_own = sb[:, :SWA_WINDOW], sb[:, SWA_WINDOW:]
                if s == 0 and j == 0:
                    s_prev = jnp.where(t != 0, s_prev, NEG_INF)
                dense = jnp.where(from_prev, s_prev, s_own)
                sink = sinks_ref[layer, (pair0 + p) * HEADS_PER_LANE_TILE + e]
                m = jnp.maximum(jnp.max(dense, axis=-1, keepdims=True), sink)
                pe = jnp.exp(dense - m)
                den = jnp.sum(pe, axis=-1, keepdims=True) + jnp.exp(sink - m)
                row_p.append(jnp.where(from_prev, pe, 0.0).astype(BF16))
                row_p.append(jnp.where(from_prev, 0.0, pe).astype(BF16))
                inv_p.append(1.0 / den)
            probs.append(jnp.concatenate(row_p, axis=1))
            inv.append(inv_p)
        o = _dot(jnp.concatenate(probs, axis=0), vcat)
        for p in range(PAIRS_PER_KV):
            scale = jnp.where(low_half, inv[p][0], inv[p][1])
            op = o[p * SWA_WINDOW:(p + 1) * SWA_WINDOW] * scale
            swa_scr[in_block(rows), (pair0 + p) * LANES:(pair0 + p + 1) * LANES] = op.astype(BF16)

    m_cols = [slice(hd * MEM_HD, (hd + 1) * MEM_HD) for hd in range(MEM_HEADS)]

    def mem_output(cols, s):
        m = jnp.max(s, axis=-1, keepdims=True)
        pe = jnp.exp(s - m)
        den = jnp.sum(pe, axis=-1, keepdims=True)
        o = _dot(pe.astype(BF16), mv_ref[:, cols]) * (1.0 / den)
        mo_scr[tile, cols] = o.astype(BF16)

    items = [(j, kh) for j in range(n_blk) for kh in range(SWA_KV_HEADS)]
    late = {}

    def mem_queries():
        mq = (proj(OFF_MQ, MEM_Q_W) * (MEM_HD ** -0.5)).astype(BF16)
        late["mem_s"] = [_dot_nt(mq[:, cols], mk_ref[:, cols]) for cols in m_cols]

    def up_ret_head(hd):
        part = _dot(ret_scr[tile, v_cols[hd]], wr_ref[v_cols[hd], :])
        late["up_ret"] = part if hd == 0 else late["up_ret"] + part

    fillers = [
        mem_queries,
        lambda: (up_ret_head(0), late.update(gate_ret=jax.nn.sigmoid(proj(OFF_GL, D_MODEL)))),
        lambda: (up_ret_head(1), late.update(gate_swa=jax.nn.sigmoid(proj(OFF_GL + D_MODEL, D_MODEL)))),
        lambda: (up_ret_head(2), late.update(gate_mem=jax.nn.sigmoid(proj(OFF_GL + 2 * D_MODEL, D_MODEL)))),
        lambda: up_ret_head(3),
    ]

    def fill():
        if fillers:
            fillers.pop(0)()

    fill()
    swa_s = {0: swa_scores(*items[0])}
    for i, (j, kh) in enumerate(items):
        if i + 1 < len(items):
            swa_s[i + 1] = swa_scores(*items[i + 1])
        fill()
        swa_output(j, kh, swa_s.pop(i))
    while fillers:
        fill()
    up_swa = []
    for i, (cols, sc) in enumerate(zip(m_cols, late["mem_s"])):
        mem_output(cols, sc)
        up_swa.append(_dot(swa_scr[tile, :], ws_ref[:, i * MXU_WIDTH:(i + 1) * MXU_WIDTH]))
    assert len(up_swa) * MXU_WIDTH == ws_ref.shape[1]

    merged = late["gate_ret"] * late["up_ret"]
    merged += late["gate_swa"] * jnp.concatenate(up_swa, axis=1)
    merged += late["gate_mem"] * _dot(mo_scr[tile, :], wm_ref[...])
    o_ref[tile, :] = x + _dot(merged.astype(BF16), wo_ref[...])


def _mix(x, batch, seq, l, cos, sin, g, w_in, decay, xi, zeta, gn, sinks, mk, mv, wr, ws, wm, wo, chunk_decay):
    n_tok, d = x.shape
    tm = MIX_TILE * MIX_TILES_PER_STEP
    steps = seq // tm
    tok = pl.BlockSpec((tm, d), lambda b, t: (b * steps + t, 0))
    pos = pl.BlockSpec((tm, RET_DK), lambda b, t: (t, 0))
    memb = pl.BlockSpec((None, MEM_LEN, MEM_Q_W), lambda b, t: (b, 0, 0))
    return pl.pallas_call(
        functools.partial(_mix_kernel, layer=l, chunk_decay=chunk_decay),
        grid=(batch, steps),
        in_specs=[pl.BlockSpec(memory_space=pltpu.SMEM), tok, pos, pos, _layer(g, l), _HBM,
                  _resident(decay.shape), _resident(xi.shape), _resident(zeta.shape), _layer(gn, l), memb, memb,
                  _HBM, _HBM, _HBM, _HBM],
        out_specs=tok,
        out_shape=jax.ShapeDtypeStruct((n_tok, d), F32),
        scratch_shapes=[pltpu.VMEM(w_in.shape[1:], BF16)] + [pltpu.VMEM((d, d), BF16)] * 4 + [
                        *_stage(d // (4 * WEIGHT_CHUNKS), w_in.shape[2]), *_stage(d // WEIGHT_CHUNKS * 2, d),
                        pltpu.VMEM((RET_HEADS, RET_DK, RET_DV), F32),
                        pltpu.VMEM((tm + SWA_WINDOW, SWA_KV_W), F32),
                        pltpu.VMEM((tm + SWA_WINDOW, SWA_KV_W), F32),
                        pltpu.VMEM((tm, RET_V_W), BF16),
                        pltpu.VMEM((tm, SWA_Q_W), BF16),
                        pltpu.VMEM((tm, MEM_Q_W), BF16)],
        compiler_params=pltpu.CompilerParams(dimension_semantics=("arbitrary", "arbitrary"),
                                             vmem_limit_bytes=VMEM_LIMIT_BYTES),
        name="mix",
    )(sinks, x, cos, sin, g, w_in, decay, xi, zeta, gn, mk, mv, wr, ws, wm, wo)


def _position_tables(seq):
    half = RET_DK // 2
    inv = ROPE_BASE ** (-jnp.arange(half, dtype=F32) / half)
    ang = jnp.arange(seq, dtype=F32)[:, None] * inv[None, :]
    cos, sin = jnp.cos(ang), jnp.sin(ang)
    cos_full = jnp.concatenate([cos, cos], axis=-1)
    sin_full = jnp.concatenate([-sin, sin], axis=-1)
    log_g = jnp.log1p(-jnp.power(2.0, -5.0 - jnp.arange(RET_HEADS, dtype=F32)))
    i = jnp.arange(RET_CHUNK, dtype=F32)
    diff = i[:, None] - i[None, :]
    decay = jnp.where(diff[None] >= 0, jnp.exp(jnp.maximum(diff, 0.0)[None] * log_g[:, None, None]), 0.0)
    zeta = jnp.exp((RET_CHUNK - 1 - i)[:, None] * log_g[None, :])
    xi = jnp.exp((i + 1)[:, None] * log_g[None, :])
    widen = lambda a: jnp.repeat(a, RET_DK, axis=1)
    return cos_full, sin_full, decay, widen(xi), widen(zeta)


def kernel(x, mem, norm_ffn1, ffn1_w1, ffn1_w3, ffn1_w2, norm_mix, w_in, ret_gn, swa_sinks, norm_mem, w_mem_kv,
           w_up_ret, w_up_swa, w_up_mem, w_o, norm_ffn2, ffn2_w1, ffn2_w3, ffn2_w2, norm_final):
    batch, seq, d = x.shape
    assert d == D_MODEL and seq % (MIX_TILE * MIX_TILES_PER_STEP) == 0
    assert (batch * seq) % (FFN_TILE * FFN_TILES_PER_STEP) == 0
    assert MIX_TILE % RET_CHUNK == 0 and MIX_TILE % SWA_WINDOW == 0
    cos, sin, decay, xi, zeta = _position_tables(seq)
    chunk_decay = tuple(math.exp(RET_CHUNK * math.log1p(-(2.0 ** (-5.0 - hd)))) for hd in range(RET_HEADS))
    row = lambda g: g.reshape(DEPTH, 1, g.shape[-1])
    ffn1 = (row(norm_ffn1), ffn1_w1, ffn1_w3, ffn1_w2)
    ffn2 = (row(norm_ffn2), ffn2_w1, ffn2_w3, ffn2_w2)
    mix_w = (w_up_ret, w_up_swa, w_up_mem, w_o)
    w_kv_b = w_mem_kv.astype(BF16)
    xf = x.reshape(batch * seq, d)
    for l in range(DEPTH):
        xf = _ffn(xf, *ffn1, l, norm_final, False)
        mk, mv = _memkv(mem, row(norm_mem), w_kv_b, l)
        xf = _mix(xf, batch, seq, l, cos, sin, row(norm_mix), w_in, decay, xi, zeta, row(ret_gn), swa_sinks,
                  mk, mv, *mix_w, chunk_decay)
        xf = _ffn(xf, *ffn2, l, norm_final, l == DEPTH - 1)
    return xf.reshape(batch, seq, d)
```

```python
import functools
import math

import jax
import jax.numpy as jnp
from jax import lax
from jax.experimental import pallas as pl
from jax.experimental.pallas import tpu as pltpu

F32 = jnp.float32
BF16 = jnp.bfloat16

D_MODEL = 1024
DEPTH = 2
MEM_LEN = 256
EPS = 1e-6
ROPE_BASE = 10000.0
RET_HEADS = 4
RET_DK = 128
RET_DV = 256
RET_CHUNK = 128
SWA_Q_HEADS = 16
SWA_KV_HEADS = 2
SWA_HD = 64
SWA_WINDOW = 128
MEM_HEADS = 4
MEM_HD = 256
D_FF = 2816
NEG_INF = -1e30

RET_QK_W = RET_HEADS * RET_DK
RET_V_W = RET_HEADS * RET_DV
SWA_Q_W = SWA_Q_HEADS * SWA_HD
SWA_KV_W = SWA_KV_HEADS * SWA_HD
MEM_Q_W = MEM_HEADS * MEM_HD
OFF_RQ = 0
OFF_RK = OFF_RQ + RET_QK_W
OFF_RV = OFF_RK + RET_QK_W
OFF_RG = OFF_RV + RET_V_W
OFF_SQ = OFF_RG + RET_V_W
OFF_SK = OFF_SQ + SWA_Q_W
OFF_SV = OFF_SK + SWA_KV_W
OFF_MQ = OFF_SV + SWA_KV_W
OFF_GL = OFF_MQ + MEM_Q_W

LANES = 128
MXU_WIDTH = 256
HEADS_PER_LANE_TILE = LANES // SWA_HD
SWA_GROUP = SWA_Q_HEADS // SWA_KV_HEADS
PAIRS_PER_KV = SWA_GROUP // HEADS_PER_LANE_TILE

FFN_TILE = 512
FFN_TILES_PER_STEP = 2
MIX_TILE = 256
MIX_TILES_PER_STEP = 2
MEMKV_TILE = 1024
VMEM_LIMIT_BYTES = 56 * 1024 * 1024
WEIGHT_CHUNKS = 8
STAGE_SLOTS = 4


def _inv_rms(x):
    return lax.rsqrt(jnp.mean(x * x, axis=-1, keepdims=True) + EPS)


def _rmsnorm(x, g):
    return x * _inv_rms(x) * g


def _dot(a, b):
    return jnp.dot(a, b, preferred_element_type=F32)


def _dot_nt(a, b):
    return lax.dot_general(a, b, (((1,), (1,)), ((), ())), preferred_element_type=F32)


def _dot_tn(a, b):
    return lax.dot_general(a, b, (((0,), (0,)), ((), ())), preferred_element_type=F32)


def _resident(shape):
    zeros = (0,) * len(shape)
    return pl.BlockSpec(shape, lambda *_: zeros, pipeline_mode=pl.Buffered(1))


def _layer(stacked, l):
    zeros = (0,) * (stacked.ndim - 1)
    return pl.BlockSpec((None,) + stacked.shape[1:], lambda *_: (l,) + zeros, pipeline_mode=pl.Buffered(1))


_HBM = pl.BlockSpec(memory_space=pl.ANY)


def _stage(rows, cols):
    return [pltpu.VMEM((STAGE_SLOTS, rows, cols), F32), pltpu.SemaphoreType.DMA((STAGE_SLOTS,))]


def _load_as_bf16(jobs):
    plan = []
    slot = 0
    for k, (src, dst, stage, sem) in enumerate(jobs):
        rows = stage.shape[1]
        assert src.shape[0] % rows == 0 and src.shape[1] == stage.shape[2]
        if k and stage is not jobs[k - 1][2]:
            slot = 0
        for row0 in range(0, src.shape[0], rows):
            plan.append((src, dst, row0, rows, stage, sem, slot))
            slot = (slot + 1) % STAGE_SLOTS

    def copy(i):
        src, _, row0, rows, stage, sem, slot = plan[i]
        return pltpu.make_async_copy(src.at[pl.ds(row0, rows), :], stage.at[slot], sem.at[slot])

    ahead = STAGE_SLOTS - 1
    for i in range(min(ahead, len(plan))):
        copy(i).start()
    for i, (_, dst, row0, rows, stage, _, slot) in enumerate(plan):
        if i + ahead < len(plan):
            copy(i + ahead).start()
        copy(i).wait()
        dst[row0:row0 + rows, :] = stage[slot].astype(BF16)


def _ffn_kernel(x_ref, g_ref, w1_hbm, w3_hbm, w2_hbm, gf_ref, o_ref,
                w1_ref, w3_ref, w2_ref, *, layer, final_norm):
    d, f = w1_ref.shape

    @pl.when(pl.program_id(0) == 0)
    def _():
        def load(stage_in, sem_in, stage_out, sem_out):
            _load_as_bf16([(w1_hbm.at[layer], w1_ref, stage_in, sem_in),
                           (w3_hbm.at[layer], w3_ref, stage_in, sem_in),
                           (w2_hbm.at[layer], w2_ref, stage_out, sem_out)])

        pl.run_scoped(load, *_stage(d // WEIGHT_CHUNKS, f), *_stage(f // WEIGHT_CHUNKS, d))

    for s in range(x_ref.shape[0] // FFN_TILE):
        rows = slice(s * FFN_TILE, (s + 1) * FFN_TILE)
        x = x_ref[rows, :]
        h = (x * g_ref[...]).astype(BF16)
        r = _inv_rms(x)
        u = []
        for c in range(0, w1_ref.shape[1], MXU_WIDTH):
            a = _dot(h, w1_ref[:, c:c + MXU_WIDTH]) * r
            b = _dot(h, w3_ref[:, c:c + MXU_WIDTH]) * r
            u.append((jax.nn.silu(a) * b).astype(BF16))
        u = jnp.concatenate(u, axis=1)
        out = x + 0.5 * _dot(u, w2_ref[...])
        if final_norm:
            out = _rmsnorm(out, gf_ref[...])
        o_ref[rows, :] = out


def _ffn(x, g, w1, w3, w2, l, g_final, final_norm):
    n_tok, d = x.shape
    block_rows = FFN_TILE * FFN_TILES_PER_STEP
    block = pl.BlockSpec((block_rows, d), lambda i: (i, 0))
    f = w1.shape[2]
    return pl.pallas_call(
        functools.partial(_ffn_kernel, layer=l, final_norm=final_norm),
        grid=(n_tok // block_rows,),
        in_specs=[block, _layer(g, l), _HBM, _HBM, _HBM, _resident((1, d))],
        out_specs=block,
        out_shape=jax.ShapeDtypeStruct((n_tok, d), F32),
        scratch_shapes=[pltpu.VMEM((d, f), BF16), pltpu.VMEM((d, f), BF16), pltpu.VMEM((f, d), BF16)],
        compiler_params=pltpu.CompilerParams(dimension_semantics=("arbitrary",),
                                             vmem_limit_bytes=VMEM_LIMIT_BYTES),
        name="ffn_final" if final_norm else "ffn",
    )(x, g, w1, w3, w2, g_final.reshape(1, d))


def _memkv_kernel(mem_ref, g_ref, w_ref, k_ref, v_ref):
    h = _rmsnorm(mem_ref[...], g_ref[...]).astype(BF16)
    kv = _dot(h, w_ref[...])
    k_ref[...] = kv[:, :MEM_Q_W].astype(BF16)
    v_ref[...] = kv[:, MEM_Q_W:].astype(BF16)


def _memkv(mem, g, w, l):
    b, m, d = mem.shape
    rows = b * m
    assert rows % MEMKV_TILE == 0
    out = jax.ShapeDtypeStruct((rows, MEM_Q_W), BF16)
    mk, mv = pl.pallas_call(
        _memkv_kernel,
        grid=(rows // MEMKV_TILE,),
        in_specs=[pl.BlockSpec((MEMKV_TILE, d), lambda i: (i, 0)), _layer(g, l), _layer(w, l)],
        out_specs=[pl.BlockSpec((MEMKV_TILE, MEM_Q_W), lambda i: (i, 0))] * 2,
        out_shape=[out, out],
        compiler_params=pltpu.CompilerParams(dimension_semantics=("arbitrary",),
                                             vmem_limit_bytes=VMEM_LIMIT_BYTES),
        name="mem_kv",
    )(mem.reshape(rows, d), g, w)
    return mk.reshape(b, m, MEM_Q_W), mv.reshape(b, m, MEM_Q_W)


def _mix_kernel(sinks_ref, x_ref, cos_ref, sin_ref, g_ref, win_hbm, decay_ref, xi_ref, zeta_ref, gn_ref,
                mk_ref, mv_ref, wr_hbm, ws_hbm, wm_hbm, wo_hbm, o_ref,
                win_ref, wr_ref, ws_ref, wm_ref, wo_ref,
                state_scr, kext_scr, vext_scr, ret_scr, swa_scr, mo_scr, *, layer, chunk_decay):
    block_rows = x_ref.shape[0]
    d, in_w = win_ref.shape

    @pl.when(jnp.logical_and(pl.program_id(0) == 0, pl.program_id(1) == 0))
    def _():
        def load(stage_in, sem_in, stage_up, sem_up):
            ups = ((wr_hbm, wr_ref), (ws_hbm, ws_ref), (wm_hbm, wm_ref), (wo_hbm, wo_ref))
            _load_as_bf16([(win_hbm.at[layer], win_ref, stage_in, sem_in)]
                          + [(src.at[layer], dst, stage_up, sem_up) for src, dst in ups])

        pl.run_scoped(load, *_stage(d // (4 * WEIGHT_CHUNKS), in_w), *_stage(d // WEIGHT_CHUNKS * 2, d))

    @pl.when(pl.program_id(1) == 0)
    def _():
        state_scr[...] = jnp.zeros_like(state_scr)
        kext_scr[0:SWA_WINDOW, :] = jnp.zeros((SWA_WINDOW, SWA_KV_W), F32)
        vext_scr[0:SWA_WINDOW, :] = jnp.zeros((SWA_WINDOW, SWA_KV_W), F32)

    for s in range(block_rows // MIX_TILE):
        _mix_tile(s, sinks_ref, x_ref, cos_ref, sin_ref, g_ref, win_ref, decay_ref, xi_ref, zeta_ref, gn_ref,
                  mk_ref, mv_ref, wr_ref, ws_ref, wm_ref, wo_ref, o_ref,
                  state_scr, kext_scr, vext_scr, ret_scr, swa_scr, mo_scr, layer=layer, chunk_decay=chunk_decay)
    kext_scr[0:SWA_WINDOW, :] = kext_scr[block_rows:block_rows + SWA_WINDOW, :]
    vext_scr[0:SWA_WINDOW, :] = vext_scr[block_rows:block_rows + SWA_WINDOW, :]


def _mix_tile(s, sinks_ref, x_ref, cos_ref, sin_ref, g_ref, win_ref, decay_ref, xi_ref, zeta_ref, gn_ref,
              mk_ref, mv_ref, wr_ref, ws_ref, wm_ref, wo_ref, o_ref,
              state_scr, kext_scr, vext_scr, ret_scr, swa_scr, mo_scr, *, layer, chunk_decay):
    t = pl.program_id(1)
    tm = MIX_TILE
    r0 = s * tm
    tile = slice(r0, r0 + tm)
    in_block = lambda rows: slice(r0 + rows.start, r0 + rows.stop)
    n_blk = tm // SWA_WINDOW

    x = x_ref[tile, :]
    h = _rmsnorm(x, g_ref[...]).astype(BF16)

    def proj(off, width):
        return _dot(h, win_ref[:, off:off + width])


    cos = cos_ref[tile, :]
    sin = sin_ref[tile, :]
    heads = range(RET_HEADS)
    chunks = [slice(c * RET_CHUNK, (c + 1) * RET_CHUNK) for c in range(tm // RET_CHUNK)]
    qk_cols = [slice(hd * RET_DK, (hd + 1) * RET_DK) for hd in heads]
    v_cols = [slice(hd * RET_DV, (hd + 1) * RET_DV) for hd in heads]

    def rotate(v):
        return v * cos + pltpu.roll(v, RET_DK // 2, 1) * sin

    rq = proj(OFF_RQ, RET_QK_W)
    rk = proj(OFF_RK, RET_QK_W)
    q = [rotate(rq[:, qk_cols[hd]]) for hd in heads]
    k = [rotate(rk[:, qk_cols[hd]]) * (RET_DK ** -0.5) for hd in heads]
    v = proj(OFF_RV, RET_V_W).astype(BF16)
    scores = [[_dot_nt(q[hd][rows].astype(BF16), k[hd][rows].astype(BF16)) for rows in chunks] for hd in heads]
    chunk_state = [[_dot_tn((k[hd][rows] * zeta_ref[:, qk_cols[hd]]).astype(BF16), v[rows, v_cols[hd]])
                    for rows in chunks] for hd in heads]
    gate = jax.nn.silu(proj(OFF_RG, RET_V_W))
    state = [state_scr[hd] for hd in heads]

    def retention_chunk(c):
        rows = chunks[c]
        for hd in heads:
            s = scores[hd][c] * decay_ref[hd]
            lhs = jnp.concatenate([s.astype(BF16), (q[hd][rows] * xi_ref[:, qk_cols[hd]]).astype(BF16)], axis=1)
            rhs = jnp.concatenate([v[rows, v_cols[hd]], state[hd].astype(BF16)], axis=0)
            o = _rmsnorm(_dot(lhs, rhs), gn_ref[:, v_cols[hd]])
            ret_scr[in_block(rows), v_cols[hd]] = (o * gate[rows, v_cols[hd]]).astype(BF16)
            state[hd] = state[hd] * chunk_decay[hd] + chunk_state[hd][c]

    retention_chunk(0)
    sq = (proj(OFF_SQ, SWA_Q_W) * (SWA_HD ** -0.5)).astype(BF16)
    skv = proj(OFF_SK, 2 * SWA_KV_W)
    kext_scr[SWA_WINDOW + r0:SWA_WINDOW + r0 + tm, :] = skv[:, :SWA_KV_W]
    vext_scr[SWA_WINDOW + r0:SWA_WINDOW + r0 + tm, :] = skv[:, SWA_KV_W:]
    m_cols = [slice(hd * MEM_HD, (hd + 1) * MEM_HD) for hd in range(MEM_HEADS)]
    mq = (proj(OFF_MQ, MEM_Q_W) * (MEM_HD ** -0.5)).astype(BF16)
    mem_s = [_dot_nt(mq[:, cols], mk_ref[:, cols]) for cols in m_cols]
    for c in range(1, len(chunks)):
        retention_chunk(c)
    for hd in heads:
        state_scr[hd] = state[hd]

    lane = lax.broadcasted_iota(jnp.int32, (1, LANES), 1)
    low_half = lane < SWA_HD
    from_prev = (lax.broadcasted_iota(jnp.int32, (SWA_WINDOW, SWA_WINDOW), 1)
                 > lax.broadcasted_iota(jnp.int32, (SWA_WINDOW, SWA_WINDOW), 0))

    def lane_halves(a, kh):
        own = jnp.where(low_half if kh == 0 else jnp.logical_not(low_half), a, 0.0)
        other = pltpu.roll(own, SWA_HD, 1)
        return (own, other) if kh == 0 else (other, own)

    def swa_scores(j, kh):
        rows = slice(j * SWA_WINDOW, (j + 1) * SWA_WINDOW)
        k_lo, k_hi = lane_halves(kext_scr[r0 + j * SWA_WINDOW:r0 + (j + 2) * SWA_WINDOW, :], kh)
        kcat = jnp.concatenate([k_lo, k_hi], axis=0).astype(BF16)
        pair0 = kh * PAIRS_PER_KV
        qs = jnp.concatenate(
            [sq[rows, (pair0 + p) * LANES:(pair0 + p + 1) * LANES] for p in range(PAIRS_PER_KV)], axis=0)
        return _dot_nt(qs, kcat)

    def swa_output(j, kh, scores_jk):
        rows = slice(j * SWA_WINDOW, (j + 1) * SWA_WINDOW)
        v_lo, v_hi = lane_halves(vext_scr[r0 + j * SWA_WINDOW:r0 + (j + 2) * SWA_WINDOW, :], kh)
        vcat = jnp.concatenate([v_lo, v_hi], axis=0).astype(BF16)
        pair0 = kh * PAIRS_PER_KV
        probs = []
        inv = []
        for p in range(PAIRS_PER_KV):
            row_p = []
            inv_p = []
            for e in range(HEADS_PER_LANE_TILE):
                sb = scores_jk[p * SWA_WINDOW:(p + 1) * SWA_WINDOW, e * 2 * SWA_WINDOW:(e + 1) * 2 * SWA_WINDOW]
                s_prev, s_own = sb[:, :SWA_WINDOW], sb[:, SWA_WINDOW:]
                if s == 0 and j == 0:
                    s_prev = jnp.where(t != 0, s_prev, NEG_INF)
                dense = jnp.where(from_prev, s_prev, s_own)
                sink = sinks_ref[layer, (pair0 + p) * HEADS_PER_LANE_TILE + e]
                m = jnp.maximum(jnp.max(dense, axis=-1, keepdims=True), sink)
                pe = jnp.exp(dense - m)
                den = jnp.sum(pe, axis=-1, keepdims=True) + jnp.exp(sink - m)
                row_p.append(jnp.where(from_prev, pe, 0.0).astype(BF16))
                row_p.append(jnp.where(from_prev, 0.0, pe).astype(BF16))
                inv_p.append(1.0 / den)
            probs.append(jnp.concatenate(row_p, axis=1))
            inv.append(inv_p)
        o = _dot(jnp.concatenate(probs, axis=0), vcat)
        for p in range(PAIRS_PER_KV):
            scale = jnp.where(low_half, inv[p][0], inv[p][1])
            op = o[p * SWA_WINDOW:(p + 1) * SWA_WINDOW] * scale
            swa_scr[in_block(rows), (pair0 + p) * LANES:(pair0 + p + 1) * LANES] = op.astype(BF16)

    def mem_output(cols, s):
        m = jnp.max(s, axis=-1, keepdims=True)
        pe = jnp.exp(s - m)
        den = jnp.sum(pe, axis=-1, keepdims=True)
        o = _dot(pe.astype(BF16), mv_ref[:, cols]) * (1.0 / den)
        mo_scr[tile, cols] = o.astype(BF16)

    items = [(j, kh) for j in range(n_blk) for kh in range(SWA_KV_HEADS)]
    late = {"mem_s": mem_s}

    def up_ret_head(hd):
        part = _dot(ret_scr[tile, v_cols[hd]], wr_ref[v_cols[hd], :])
        late["up_ret"] = part if hd == 0 else late["up_ret"] + part

    fillers = [
        lambda: (up_ret_head(0), late.update(gate_ret=jax.nn.sigmoid(proj(OFF_GL, D_MODEL)))),
        lambda: (up_ret_head(1), late.update(gate_swa=jax.nn.sigmoid(proj(OFF_GL + D_MODEL, D_MODEL)))),
        lambda: (up_ret_head(2), late.update(gate_mem=jax.nn.sigmoid(proj(OFF_GL + 2 * D_MODEL, D_MODEL)))),
        lambda: up_ret_head(3),
    ]

    def fill():
        if fillers:
            fillers.pop(0)()

    swa_s = {0: swa_scores(*items[0])}
    for i, (j, kh) in enumerate(items):
        if i + 1 < len(items):
            swa_s[i + 1] = swa_scores(*items[i + 1])
        fill()
        swa_output(j, kh, swa_s.pop(i))
    while fillers:
        fill()
    up_swa = []
    for i, (cols, sc) in enumerate(zip(m_cols, late["mem_s"])):
        mem_output(cols, sc)
        up_swa.append(_dot(swa_scr[tile, :], ws_ref[:, i * MXU_WIDTH:(i + 1) * MXU_WIDTH]))
    assert len(up_swa) * MXU_WIDTH == ws_ref.shape[1]

    merged = late["gate_ret"] * late["up_ret"]
    merged += late["gate_swa"] * jnp.concatenate(up_swa, axis=1)
    merged += late["gate_mem"] * _dot(mo_scr[tile, :], wm_ref[...])
    o_ref[tile, :] = x + _dot(merged.astype(BF16), wo_ref[...])


def _mix(x, batch, seq, l, cos, sin, g, w_in, decay, xi, zeta, gn, sinks, mk, mv, wr, ws, wm, wo, chunk_decay):
    n_tok, d = x.shape
    tm = MIX_TILE * MIX_TILES_PER_STEP
    steps = seq // tm
    tok = pl.BlockSpec((tm, d), lambda b, t: (b * steps + t, 0))
    pos = pl.BlockSpec((tm, RET_DK), lambda b, t: (t, 0))
    memb = pl.BlockSpec((None, MEM_LEN, MEM_Q_W), lambda b, t: (b, 0, 0))
    return pl.pallas_call(
        functools.partial(_mix_kernel, layer=l, chunk_decay=chunk_decay),
        grid=(batch, steps),
        in_specs=[pl.BlockSpec(memory_space=pltpu.SMEM), tok, pos, pos, _layer(g, l), _HBM,
                  _resident(decay.shape), _resident(xi.shape), _resident(zeta.shape), _layer(gn, l), memb, memb,
                  _HBM, _HBM, _HBM, _HBM],
        out_specs=tok,
        out_shape=jax.ShapeDtypeStruct((n_tok, d), F32),
        scratch_shapes=[pltpu.VMEM(w_in.shape[1:], BF16)] + [pltpu.VMEM((d, d), BF16)] * 4 + [
                        pltpu.VMEM((RET_HEADS, RET_DK, RET_DV), F32),
                        pltpu.VMEM((tm + SWA_WINDOW, SWA_KV_W), F32),
                        pltpu.VMEM((tm + SWA_WINDOW, SWA_KV_W), F32),
                        pltpu.VMEM((tm, RET_V_W), BF16),
                        pltpu.VMEM((tm, SWA_Q_W), BF16),
                        pltpu.VMEM((tm, MEM_Q_W), BF16)],
        compiler_params=pltpu.CompilerParams(dimension_semantics=("arbitrary", "arbitrary"),
                                             vmem_limit_bytes=VMEM_LIMIT_BYTES),
        name="mix",
    )(sinks, x, cos, sin, g, w_in, decay, xi, zeta, gn, mk, mv, wr, ws, wm, wo)


def _position_tables(seq):
    half = RET_DK // 2
    inv = ROPE_BASE ** (-jnp.arange(half, dtype=F32) / half)
    ang = jnp.arange(seq, dtype=F32)[:, None] * inv[None, :]
    cos, sin = jnp.cos(ang), jnp.sin(ang)
    cos_full = jnp.concatenate([cos, cos], axis=-1)
    sin_full = jnp.concatenate([-sin, sin], axis=-1)
    log_g = jnp.log1p(-jnp.power(2.0, -5.0 - jnp.arange(RET_HEADS, dtype=F32)))
    i = jnp.arange(RET_CHUNK, dtype=F32)
    diff = i[:, None] - i[None, :]
    decay = jnp.where(diff[None] >= 0, jnp.exp(jnp.maximum(diff, 0.0)[None] * log_g[:, None, None]), 0.0)
    zeta = jnp.exp((RET_CHUNK - 1 - i)[:, None] * log_g[None, :])
    xi = jnp.exp((i + 1)[:, None] * log_g[None, :])
    widen = lambda a: jnp.repeat(a, RET_DK, axis=1)
    return cos_full, sin_full, decay, widen(xi), widen(zeta)


def kernel(x, mem, norm_ffn1, ffn1_w1, ffn1_w3, ffn1_w2, norm_mix, w_in, ret_gn, swa_sinks, norm_mem, w_mem_kv,
           w_up_ret, w_up_swa, w_up_mem, w_o, norm_ffn2, ffn2_w1, ffn2_w3, ffn2_w2, norm_final):
    batch, seq, d = x.shape
    assert d == D_MODEL and seq % (MIX_TILE * MIX_TILES_PER_STEP) == 0
    assert (batch * seq) % (FFN_TILE * FFN_TILES_PER_STEP) == 0
    assert MIX_TILE % RET_CHUNK == 0 and MIX_TILE % SWA_WINDOW == 0
    cos, sin, decay, xi, zeta = _position_tables(seq)
    chunk_decay = tuple(math.exp(RET_CHUNK * math.log1p(-(2.0 ** (-5.0 - hd)))) for hd in range(RET_HEADS))
    row = lambda g: g.reshape(DEPTH, 1, g.shape[-1])
    ffn1 = (row(norm_ffn1), ffn1_w1, ffn1_w3, ffn1_w2)
    ffn2 = (row(norm_ffn2), ffn2_w1, ffn2_w3, ffn2_w2)
    mix_w = (w_up_ret, w_up_swa, w_up_mem, w_o)
    w_kv_b = w_mem_kv.astype(BF16)
    xf = x.reshape(batch * seq, d)
    for l in range(DEPTH):
        xf = _ffn(xf, *ffn1, l, norm_final, False)
        mk, mv = _memkv(mem, row(norm_mem), w_kv_b, l)
        xf = _mix(xf, batch, seq, l, cos, sin, row(norm_mix), w_in, decay, xi, zeta, row(ret_gn), swa_sinks,
                  mk, mv, *mix_w, chunk_decay)
        xf = _ffn(xf, *ffn2, l, norm_final, l == DEPTH - 1)
    return xf.reshape(batch, seq, d)
```
